```python
import math
import functools
import jax
import jax.numpy as jnp
from jax import lax
import numpy as np

D_MODEL = 1024
BATCH = 8
SEQ = 2048
DEPTH = 2
DEC_BATCH = 128
DEC_SEQ = 4
PAST_LEN = 16384
PAGE_SIZE = 128

A_HEADS = 4
A_KV_HEADS = 1
A_HEAD_DIM = 64
A_V_DIM = 2 * A_HEAD_DIM
B_HEADS = 8
B_Q_LORA = 256
B_KV_LORA = 128
B_NOPE_DIM = 64
B_ROPE_DIM = 32
B_V_DIM = 64
C_HEADS = 8
C_KV_HEADS = 2
C_HEAD_DIM = 64
N_EXPERTS = 64
TOP_K = 8
N_GROUPS = 8
TOPK_GROUPS = 4
EXPERT_DIM = 256
SHARED_DIM = 256
ROUTED_SCALE = 2.5
ROPE_THETA = 10000.0
QBLOCK = 128
RMS_EPS = 1e-6

kernel_name = 'hybrid_diff_mla_fox_moe_step'


def _in_sizes():
    return (2 * A_HEADS * A_HEAD_DIM,
            2 * A_KV_HEADS * A_HEAD_DIM,
            A_KV_HEADS * A_V_DIM,
            B_Q_LORA,
            B_KV_LORA,
            B_ROPE_DIM,
            C_HEADS * C_HEAD_DIM,
            C_KV_HEADS * C_HEAD_DIM,
            C_KV_HEADS * C_HEAD_DIM,
            C_HEADS,
            3 * D_MODEL)


def _rmsnorm(x, g):
    xf = x.astype(jnp.float32)
    y = xf * lax.rsqrt(jnp.mean(xf * xf, axis=-1, keepdims=True) + RMS_EPS)
    return (y * g.astype(jnp.float32)).astype(x.dtype)


def _rope(x, pos):
    half = x.shape[-1] // 2
    inv = ROPE_THETA ** (-jnp.arange(half, dtype=jnp.float32) / half)
    ang = pos.astype(jnp.float32)[:, None] * inv[None, :]
    bshape = (pos.shape[0],) + (1,) * (x.ndim - 3) + (half,)
    cos = jnp.cos(ang).reshape(bshape)
    sin = jnp.sin(ang).reshape(bshape)
    xf = x.astype(jnp.float32)
    x1, x2 = xf[..., :half], xf[..., half:]
    return jnp.concatenate([x1 * cos - x2 * sin, x2 * cos + x1 * sin], axis=-1).astype(x.dtype)


def _probs(q, k, scale, mask, bias=None):
    s = jnp.einsum('bqhgd,bkhd->bhgqk', q, k, preferred_element_type=jnp.float32) * scale
    if bias is not None:
        s = s + bias
    s = jnp.where(mask, s, -jnp.inf)
    return jax.nn.softmax(s, axis=-1)


def _causal_blocks(attend, q_side, k_side, q_off):
    tq = q_side[0].shape[1]
    outs = []
    for start in range(0, tq, QBLOCK):
        end = min(start + QBLOCK, tq)
        kend = q_off + end
        q_pos = q_off + jnp.arange(start, end)
        k_pos = jnp.arange(kend)
        mask = q_pos[:, None] >= k_pos[None, :]
        outs.append(attend(tuple(a[:, start:end] for a in q_side),
                           tuple(a[:, :kend] for a in k_side), mask))
    return jnp.concatenate(outs, axis=1)


def _diff_attend(qs, ks, mask, lam, gain, lam_init):
    q1, q2 = qs
    k1, k2, v = ks
    scale = A_HEAD_DIM ** -0.5
    p = _probs(q1, k1, scale, mask) - lam * _probs(q2, k2, scale, mask)
    o = jnp.einsum('bhgqk,bkhd->bqhgd', p.astype(v.dtype), v)
    o = _rmsnorm(o, gain) * (1.0 - lam_init)
    return o.reshape(o.shape[0], o.shape[1], A_HEADS * A_V_DIM)


def _mla_attend(qs, ks, mask, w_uv):
    (q,) = qs
    k, v = ks
    p = _probs(q, k, (B_NOPE_DIM + B_ROPE_DIM) ** -0.5, mask)
    o_lat = jnp.einsum('bhgqk,bkhc->bqhgc', p.astype(v.dtype), v)[:, :, 0]
    o = jnp.einsum('bqhc,chv->bqhv', o_lat, w_uv)
    return o.reshape(o.shape[0], o.shape[1], B_HEADS * B_V_DIM)


def _fox_attend(qs, ks, mask):
    q, fq = qs
    k, v, fk = ks
    bias = jnp.moveaxis(fq, 1, -1)[..., :, None] - jnp.moveaxis(fk, 1, -1)[..., None, :]
    p = _probs(q, k, C_HEAD_DIM ** -0.5, mask, bias)
    o = jnp.einsum('bhgqk,bkhd->bqhgd', p.astype(v.dtype), v)
    return o.reshape(o.shape[0], o.shape[1], C_HEADS * C_HEAD_DIM)


def _moe(h, lp):
    t = h.shape[0]
    scores = jax.nn.sigmoid(jnp.einsum('td,de->te', h, lp['w_router'],
                                       preferred_element_type=jnp.float32))
    biased = scores + lp['e_bias'].astype(jnp.float32)
    grp = biased.reshape(t, N_GROUPS, N_EXPERTS // N_GROUPS)
    grp_score = jnp.sum(lax.top_k(grp, 2)[0], axis=-1)
    _, grp_idx = lax.top_k(grp_score, TOPK_GROUPS)
    grp_mask = jnp.any(grp_idx[..., None] == jnp.arange(N_GROUPS), axis=-2)
    expert_mask = jnp.repeat(grp_mask, N_EXPERTS // N_GROUPS, axis=-1)
    _, idx = lax.top_k(jnp.where(expert_mask, biased, -jnp.inf), TOP_K)
    w = jnp.take_along_axis(scores, idx, axis=-1)
    w = w / jnp.sum(w, axis=-1, keepdims=True) * ROUTED_SCALE
    combine = jnp.sum(jax.nn.one_hot(idx, N_EXPERTS, dtype=jnp.float32) * w[..., None], axis=1)
    hg = jnp.einsum('td,edf->tef', h, lp['w_gate'])
    hu = jnp.einsum('td,edf->tef', h, lp['w_up'])
    act = jax.nn.silu(hg) * hu * combine[..., None].astype(h.dtype)
    routed = jnp.einsum('tef,efd->td', act, lp['w_down'])
    shared = (jax.nn.silu(h @ lp['ws_gate']) * (h @ lp['ws_up'])) @ lp['ws_down']
    return routed + shared


def _gather_pages(cache, layer, page_table):
    g = cache[layer, page_table]
    return g.reshape((g.shape[0], g.shape[1] * g.shape[2]) + g.shape[3:])


def _layer(x, c, pos, past, lp, lam_init):
    b, t, _ = x.shape
    q_off = 0 if past is None else past['a_k'].shape[1]

    def with_past(name, new):
        if past is None:
            return new
        return jnp.concatenate([past[name].astype(new.dtype), new], axis=1)

    mod = jnp.einsum('bd,de->be', jax.nn.silu(c), lp['w_ada']) + lp['b_ada']
    sh1, sc1, g1, sh2, sc2, g2 = (m[:, None, :] for m in jnp.split(mod, 6, axis=-1))

    h = _rmsnorm(x, lp['g_mix']) * (1.0 + sc1) + sh1
    z = jnp.einsum('btd,de->bte', h, lp['w_in'])
    splits = np.cumsum(_in_sizes())[:-1].tolist()
    aq, ak, av, bcq, bckv, bkr, cq, ck, cv, cf, gl = jnp.split(z, splits, axis=-1)

    ga_ = A_HEADS // A_KV_HEADS
    aq = _rope(aq.reshape(b, t, 2, A_HEADS, A_HEAD_DIM), pos)
    a_k = _rope(ak.reshape(b, t, 2, A_KV_HEADS, A_HEAD_DIM), pos)
    a_v = av.reshape(b, t, A_KV_HEADS, A_V_DIM)
    lv = lp['a_lambda'].astype(jnp.float32)
    lam = jnp.exp(jnp.sum(lv[0] * lv[1])) - jnp.exp(jnp.sum(lv[2] * lv[3])) + lam_init
    ak_all = with_past('a_k', a_k)
    q_side = (aq[:, :, 0].reshape(b, t, A_KV_HEADS, ga_, A_HEAD_DIM),
              aq[:, :, 1].reshape(b, t, A_KV_HEADS, ga_, A_HEAD_DIM))
    k_side = (ak_all[:, :, 0], ak_all[:, :, 1], with_past('a_v', a_v))
    o_a = _causal_blocks(functools.partial(_diff_attend, lam=lam, gain=lp['a_subln'], lam_init=lam_init),
                         q_side, k_side, q_off)

    qb = jnp.einsum('btr,rhe->bthe', _rmsnorm(bcq, lp['g_bq']), lp['w_uq'])
    q_rope = _rope(qb[..., B_NOPE_DIM:], pos)
    q_lat = jnp.einsum('bthn,chn->bthc', qb[..., :B_NOPE_DIM], lp['w_uk'])
    q_b = jnp.concatenate([q_lat, q_rope], axis=-1)[:, :, None]
    b_ckv = _rmsnorm(bckv, lp['g_bkv'])
    b_kr = _rope(bkr[:, :, None, :], pos)[:, :, 0]
    ckv_all = with_past('b_ckv', b_ckv)
    k_b = jnp.concatenate([ckv_all, with_past('b_kr', b_kr)], axis=-1)[:, :, None]
    o_b = _causal_blocks(functools.partial(_mla_attend, w_uv=lp['w_uv']),
                         (q_b,), (k_b, ckv_all[:, :, None]), q_off)

    gc_ = C_HEADS // C_KV_HEADS
    c_k = ck.reshape(b, t, C_KV_HEADS, C_HEAD_DIM)
    c_v = cv.reshape(b, t, C_KV_HEADS, C_HEAD_DIM)
    c_logf = jax.nn.log_sigmoid(cf.astype(jnp.float32) + lp['b_f'].astype(jnp.float32))
    f_cum = jnp.cumsum(with_past('c_logf', c_logf), axis=1)
    tk = f_cum.shape[1]
    fq = f_cum[:, tk - t:].reshape(b, t, C_KV_HEADS, gc_)
    fk = f_cum.reshape(b, tk, C_KV_HEADS, gc_)
    o_c = _causal_blocks(_fox_attend,
                         (cq.reshape(b, t, C_KV_HEADS, gc_, C_HEAD_DIM), fq),
                         (with_past('c_k', c_k), with_past('c_v', c_v), fk), q_off)

    ga, gb, gc = jnp.split(jax.nn.sigmoid(gl), 3, axis=-1)
    merged = ga * (o_a @ lp['w_oa']) + gb * (o_b @ lp['w_ob']) + gc * (o_c @ lp['w_oc'])
    x = x + g1 * (merged @ lp['w_out'])

    h2 = _rmsnorm(x, lp['g_ffn']) * (1.0 + sc2) + sh2
    x = x + g2 * _moe(h2.reshape(b * t, -1), lp).reshape(b, t, -1)
    return x, (a_k, a_v, b_ckv, b_kr, c_k, c_v, c_logf)


def setup_inputs(seed: int = 0) -> dict:
    key = jax.random.key(seed)
    keys = iter(jax.random.split(key, 64))

    def nrm(shape, scale=1.0):
        return scale * jax.random.normal(next(keys), shape, jnp.float32)

    def gain(shape):
        return 1.0 + nrm(shape, 0.05)

    d = D_MODEL
    n_pages = PAST_LEN // PAGE_SIZE
    n_used = DEC_BATCH * n_pages
    n_pool = n_used + max(1, n_used // 4)
    page_table = jax.random.permutation(next(keys), n_pool)[:n_used].reshape(DEC_BATCH, n_pages).astype(jnp.int32)
    pool = (DEPTH, n_pool, PAGE_SIZE)
    in_width = sum(_in_sizes())
    return {
        'x_prompt': nrm((BATCH, SEQ, d)),
        'x_sample': nrm((DEC_BATCH, DEC_SEQ, d)),
        'cache_a_k': nrm(pool + (2, A_KV_HEADS, A_HEAD_DIM)),
        'cache_a_v': nrm(pool + (A_KV_HEADS, A_V_DIM)),
        'cache_b_ckv': nrm(pool + (B_KV_LORA,)),
        'cache_b_krope': nrm(pool + (B_ROPE_DIM,)),
        'cache_c_k': nrm(pool + (C_KV_HEADS, C_HEAD_DIM)),
        'cache_c_v': nrm(pool + (C_KV_HEADS, C_HEAD_DIM)),
        'cache_c_logf': jax.nn.log_sigmoid(2.5 + nrm(pool + (C_HEADS,))),
        'page_table': page_table,
        'c_prompt': nrm((BATCH, d)),
        'c_sample': nrm((DEC_BATCH, d)),
        'w_ada': nrm((DEPTH, d, 6 * d), 0.5 * d ** -0.5),
        'b_ada': nrm((DEPTH, 6 * d), 0.01),
        'g_mix': gain((DEPTH, d)),
        'w_in': nrm((DEPTH, d, in_width), d ** -0.5),
        'a_lambda': nrm((DEPTH, 4, A_HEAD_DIM), 0.1),
        'a_subln': gain((DEPTH, A_V_DIM)),
        'g_bq': gain((DEPTH, B_Q_LORA)),
        'w_uq': nrm((DEPTH, B_Q_LORA, B_HEADS, B_NOPE_DIM + B_ROPE_DIM), B_Q_LORA ** -0.5),
        'w_uk': nrm((DEPTH, B_KV_LORA, B_HEADS, B_NOPE_DIM), B_KV_LORA ** -0.5),
        'w_uv': nrm((DEPTH, B_KV_LORA, B_HEADS, B_V_DIM), B_KV_LORA ** -0.5),
        'g_bkv': gain((DEPTH, B_KV_LORA)),
        'b_f': jax.random.uniform(next(keys), (DEPTH, C_HEADS), jnp.float32, 1.0, 4.0),
        'w_oa': nrm((DEPTH, A_HEADS * A_V_DIM, d), (A_HEADS * A_V_DIM) ** -0.5),
        'w_ob': nrm((DEPTH, B_HEADS * B_V_DIM, d), (B_HEADS * B_V_DIM) ** -0.5),
        'w_oc': nrm((DEPTH, C_HEADS * C_HEAD_DIM, d), (C_HEADS * C_HEAD_DIM) ** -0.5),
        'w_out': nrm((DEPTH, d, d), d ** -0.5),
        'g_ffn': gain((DEPTH, d)),
        'w_router': nrm((DEPTH, d, N_EXPERTS), d ** -0.5),
        'e_bias': nrm((DEPTH, N_EXPERTS), 0.01),
        'w_gate': nrm((DEPTH, N_EXPERTS, d, EXPERT_DIM), d ** -0.5),
        'w_up': nrm((DEPTH, N_EXPERTS, d, EXPERT_DIM), d ** -0.5),
        'w_down': nrm((DEPTH, N_EXPERTS, EXPERT_DIM, d), EXPERT_DIM ** -0.5),
        'ws_gate': nrm((DEPTH, d, SHARED_DIM), d ** -0.5),
        'ws_up': nrm((DEPTH, d, SHARED_DIM), d ** -0.5),
        'ws_down': nrm((DEPTH, SHARED_DIM, d), SHARED_DIM ** -0.5),
        'g_final': gain((d,)),
    }


def reference(x_prompt, x_sample, cache_a_k, cache_a_v, cache_b_ckv, cache_b_krope,
              cache_c_k, cache_c_v, cache_c_logf, page_table, c_prompt, c_sample,
              w_ada, b_ada, g_mix, w_in, a_lambda, a_subln, g_bq, w_uq, w_uk, w_uv,
              g_bkv, b_f, w_oa, w_ob, w_oc, w_out, g_ffn, w_router, e_bias,
              w_gate, w_up, w_down, ws_gate, ws_up, ws_down, g_final):
    past_len = page_table.shape[1] * cache_a_k.shape[2]
    pos_p = jnp.arange(x_prompt.shape[1])
    pos_s = past_len + jnp.arange(x_sample.shape[1])
    hp, hs = x_prompt, x_sample
    rows_p, rows_s = [], []
    for l in range(DEPTH):
        lam_init = 0.8 - 0.6 * math.exp(-0.3 * l)
        lp = {'w_ada': w_ada[l], 'b_ada': b_ada[l], 'g_mix': g_mix[l], 'w_in': w_in[l],
              'a_lambda': a_lambda[l], 'a_subln': a_subln[l], 'g_bq': g_bq[l],
              'w_uq': w_uq[l], 'w_uk': w_uk[l], 'w_uv': w_uv[l], 'g_bkv': g_bkv[l],
              'b_f': b_f[l], 'w_oa': w_oa[l], 'w_ob': w_ob[l], 'w_oc': w_oc[l],
              'w_out': w_out[l], 'g_ffn': g_ffn[l], 'w_router': w_router[l],
              'e_bias': e_bias[l], 'w_gate': w_gate[l], 'w_up': w_up[l],
              'w_down': w_down[l], 'ws_gate': ws_gate[l], 'ws_up': ws_up[l],
              'ws_down': ws_down[l]}
        past = {'a_k': _gather_pages(cache_a_k, l, page_table),
                'a_v': _gather_pages(cache_a_v, l, page_table),
                'b_ckv': _gather_pages(cache_b_ckv, l, page_table),
                'b_kr': _gather_pages(cache_b_krope, l, page_table),
                'c_k': _gather_pages(cache_c_k, l, page_table),
                'c_v': _gather_pages(cache_c_v, l, page_table),
                'c_logf': _gather_pages(cache_c_logf, l, page_table)}
        hp, rp = _layer(hp, c_prompt, pos_p, None, lp, lam_init)
        hs, rs = _layer(hs, c_sample, pos_s, past, lp, lam_init)
        rows_p.append(rp)
        rows_s.append(rs)
    y_prompt = _rmsnorm(hp, g_final)
    y_sample = _rmsnorm(hs, g_final)
    a_k_p, a_v_p, b_ckv_p, b_krope_p, c_k_p, c_v_p, c_logf_p = [jnp.stack(r) for r in zip(*rows_p)]
    a_k_s, a_v_s, b_ckv_s, b_krope_s, c_k_s, c_v_s, c_logf_s = [jnp.stack(r) for r in zip(*rows_s)]
    return (y_prompt, y_sample,
            a_k_p, a_v_p, b_ckv_p, b_krope_p, c_k_p, c_v_p, c_logf_p,
            a_k_s, a_v_s, b_ckv_s, b_krope_s, c_k_s, c_v_s, c_logf_s)
```

```python
import functools
import math

import numpy as np
import jax
import jax.numpy as jnp
from jax import lax
from jax.experimental import pallas as pl
from jax.experimental.pallas import tpu as pltpu

F32 = jnp.float32
BF16 = jnp.bfloat16
HIGHEST = lax.Precision.HIGHEST

A_HEADS, A_HEAD_DIM, A_V_DIM = 4, 64, 128
B_HEADS, B_Q_LORA, B_KV_LORA, B_NOPE_DIM, B_ROPE_DIM, B_V_DIM = 8, 256, 128, 64, 32, 64
C_HEADS, C_KV_HEADS, C_HEAD_DIM = 8, 2, 64
N_EXPERTS, TOP_K, N_GROUPS, TOPK_GROUPS, EXPERT_DIM = 64, 8, 8, 4, 256
ROUTED_SCALE = 2.5
ROPE_THETA = 10000.0
SCALE_B = (B_NOPE_DIM + B_ROPE_DIM) ** -0.5
RMS_EPS = 1e-6

LANES = 128
VMEM_LIMIT = 56 * 1024 * 1024

_AQ, _AK, _AV, _BCQ, _BCKV, _CQ, _CK, _CV, _MISC, _GL = (
    0, 512, 640, 768, 1024, 1152, 1664, 1792, 1920, 2048)
_IN_COLS = 5120

_NT = (((1,), (1,)), ((), ()))


def _cparams(sem):
    return pltpu.CompilerParams(dimension_semantics=sem, vmem_limit_bytes=VMEM_LIMIT)


def _rms(x):
    return x * lax.rsqrt(jnp.mean(x * x, axis=-1, keepdims=True) + RMS_EPS)


def _log_sigmoid(x):
    return jnp.minimum(x, 0.0) - jnp.log1p(jnp.exp(-jnp.abs(x)))


def _sigmoid(x):
    return 1.0 / (1.0 + jnp.exp(-x))


def _silu(x):
    return x * _sigmoid(x)


def _mod_kernel(c_ref, w_ref, b_ref, o_ref):
    c = _silu(c_ref[...]).astype(BF16)
    o_ref[...] = jnp.dot(c, w_ref[...].astype(BF16), preferred_element_type=F32) + b_ref[...]


def _mod_call(c_all, w_ada, b_ada):
    depth, d, d6 = w_ada.shape
    nc = c_all.shape[0]
    tn = 1024
    return pl.pallas_call(
        _mod_kernel,
        grid=(depth, d6 // tn),
        in_specs=[pl.BlockSpec((nc, d), lambda l, j: (0, 0)),
                  pl.BlockSpec((None, d, tn), lambda l, j: (l, 0, j)),
                  pl.BlockSpec((None, 1, tn), lambda l, j: (l, 0, j))],
        out_specs=pl.BlockSpec((None, nc, tn), lambda l, j: (l, 0, j)),
        out_shape=jax.ShapeDtypeStruct((depth, nc, d6), F32),
        compiler_params=_cparams(("arbitrary", "arbitrary")),
        name="ada_mod",
    )(c_all, w_ada, b_ada.reshape(depth, 1, d6))


def _rope_tiles(z, tab_ref, half):
    cos = tab_ref[:, 0:LANES]
    s_lo = tab_ref[:, LANES:2 * LANES]
    s_hi = tab_ref[:, 2 * LANES:3 * LANES]
    outs = []
    for i in range(z.shape[1] // LANES):
        t = z[:, i * LANES:(i + 1) * LANES]
        outs.append(t * cos + pltpu.roll(t, LANES - half, 1) * s_lo + pltpu.roll(t, half, 1) * s_hi)
    return outs


def _inproj_kernel(x_ref, sh_ref, sc_ref, g_ref, w_ref, wcft_ref, bf_ref, bft_ref, t64_ref, t32_ref,
                   gbq_ref, wuqn_ref, wuqr_ref, wuk_ref, gbkv_ref,
                   aq_o, ak_o, av_o, qb_o, ckv_o, kr_o, cq_o, ck_o, cv_o, lf_o, lft_o, gl_o,
                   akb_o, avb_o, kbb_o, ckb_o, cvb_o):
    h = _rms(x_ref[...]) * g_ref[...] * (1.0 + sc_ref[...]) + sh_ref[...]
    hb = h.astype(BF16)

    def seg(lo, n):
        return jnp.dot(hb, w_ref[:, lo:lo + n], preferred_element_type=F32)

    aq = _rope_tiles(seg(_AQ, 512), t64_ref, A_HEAD_DIM // 2)
    for i, t in enumerate(aq):
        aq_o[:, i * LANES:(i + 1) * LANES] = (t * (A_HEAD_DIM ** -0.5)).astype(BF16)
    ak = _rope_tiles(seg(_AK, 128), t64_ref, A_HEAD_DIM // 2)[0]
    ak_o[...] = ak
    akb_o[...] = ak.astype(BF16)
    av = seg(_AV, 128)
    av_o[...] = av
    avb_o[...] = av.astype(BF16)

    qn = (_rms(seg(_BCQ, B_Q_LORA)) * gbq_ref[...]).astype(BF16)
    nope = jnp.dot(qn, wuqn_ref[...], preferred_element_type=F32).astype(BF16)
    qrope = _rope_tiles(jnp.dot(qn, wuqr_ref[...], preferred_element_type=F32), t32_ref, B_ROPE_DIM // 2)
    for hd in range(B_HEADS):
        qlat = jnp.dot(nope[:, hd * LANES:(hd + 1) * LANES], wuk_ref[hd], preferred_element_type=F32)
        qb_o[:, hd * 256:hd * 256 + LANES] = qlat.astype(BF16)
        qb_o[:, hd * 256 + LANES:(hd + 1) * 256] = qrope[hd].astype(BF16)
    ckv = _rms(seg(_BCKV, B_KV_LORA)) * gbkv_ref[...]
    ckv_o[...] = ckv
    misc = _rope_tiles(seg(_MISC, LANES), t32_ref, B_ROPE_DIM // 2)[0]
    kr_o[...] = misc[:, 0:B_ROPE_DIM]
    lane = lax.broadcasted_iota(jnp.int32, misc.shape, 1)
    kbb_o[:, 0:LANES] = ckv.astype(BF16)
    kbb_o[:, LANES:2 * LANES] = jnp.where(lane < B_ROPE_DIM, misc, 0.0).astype(BF16)

    cq = seg(_CQ, 512)
    cq_o[...] = (cq * (C_HEAD_DIM ** -0.5)).astype(BF16)
    ck = seg(_CK, 128)
    ck_o[...] = ck
    ckb_o[...] = ck.astype(BF16)
    cv = seg(_CV, 128)
    cv_o[...] = cv
    cvb_o[...] = cv.astype(BF16)
    lf_o[...] = _log_sigmoid(misc[:, B_ROPE_DIM:B_ROPE_DIM + C_HEADS] + bf_ref[...])
    cft = lax.dot_general(wcft_ref[...], hb, _NT, preferred_element_type=F32)
    lft_o[...] = _log_sigmoid(cft + bft_ref[...])

    for i in range(3):
        gl_o[:, i * 1024:(i + 1) * 1024] = _sigmoid(seg(_GL + i * 1024, 1024))


def _row_spec(tm, width):
    return pl.BlockSpec((tm, width), lambda i: (i, 0))


def _full_spec(shape):
    nd = len(shape)
    return pl.BlockSpec(shape, lambda i: (0,) * nd)


def _mod_spec(mod_arr, tm, tiles_per_seq):
    d = mod_arr.shape[-1]
    if mod_arr.ndim == 3:
        return pl.BlockSpec((None, 1, d), lambda i: (i // tiles_per_seq, 0, 0))
    return pl.BlockSpec((tm, d), lambda i: (i, 0))


def _tab_spec(tab, tm, tiles_per_seq):
    if tiles_per_seq is None:
        return pl.BlockSpec((tm, tab.shape[1]), lambda i: (i, 0))
    return pl.BlockSpec((tm, tab.shape[1]), lambda i: (i % tiles_per_seq, 0))


def _inproj_call(x, sh, sc, lw, t64, t32, tm, tiles_per_seq):
    n, d = x.shape
    outs = [("aq", 512, BF16), ("ak", 128, F32), ("av", 128, F32), ("qb", 2048, BF16), ("ckv", 128, F32),
            ("kr", 32, F32), ("cq", 512, BF16), ("ck", 128, F32), ("cv", 128, F32), ("lf", 8, F32),
            ("lft", None, F32), ("gl", 3072, F32), ("akb", 128, BF16), ("avb", 128, BF16),
            ("kbb", 256, BF16), ("ckb", 128, BF16), ("cvb", 128, BF16)]
    out_shape, out_specs = [], []
    for name, width, dt in outs:
        if name == "lft":
            out_shape.append(jax.ShapeDtypeStruct((C_HEADS, n), dt))
            out_specs.append(pl.BlockSpec((C_HEADS, tm), lambda i: (0, i)))
        else:
            out_shape.append(jax.ShapeDtypeStruct((n, width), dt))
            out_specs.append(_row_spec(tm, width))
    tps = tiles_per_seq
    in_specs = [_row_spec(tm, d), _mod_spec(sh, tm, tps), _mod_spec(sc, tm, tps), _full_spec((1, d)),
                _full_spec(lw["w_in"].shape), _full_spec(lw["w_cft"].shape), _full_spec((1, C_HEADS)),
                _full_spec((C_HEADS, 1)), _tab_spec(t64, tm, tps), _tab_spec(t32, tm, tps),
                _full_spec((1, B_Q_LORA)), _full_spec(lw["w_uqn"].shape), _full_spec(lw["w_uqr"].shape),
                _full_spec(lw["w_uk"].shape), _full_spec((1, B_KV_LORA))]
    res = pl.pallas_call(
        _inproj_kernel,
        grid=(n // tm,),
        in_specs=in_specs,
        out_specs=out_specs,
        out_shape=out_shape,
        compiler_params=_cparams(("arbitrary",)),
        name="in_proj",
    )(x, sh, sc, lw["g_mix"], lw["w_in"], lw["w_cft"], lw["b_f"], lw["b_ft"], t64, t32,
      lw["g_bq"], lw["w_uqn"], lw["w_uqr"], lw["w_uk"], lw["g_bkv"])
    return {name: r for (name, _, _), r in zip(outs, res)}


def _lambda(alam_ref, lam_init):
    a = alam_ref[...]
    s01 = jnp.sum(a[0:1] * a[1:2], axis=1, keepdims=True)
    s23 = jnp.sum(a[2:3] * a[3:4], axis=1, keepdims=True)
    return jnp.exp(s01) - jnp.exp(s23) + lam_init


def _diff_finish(o1, o2, lam, subln, lam_init):
    d = o1 - lam * o2
    return _rms(d) * subln * (1.0 - lam_init)


def _flash_kernel(*refs, mode, tq, seq, lam_init):
    if mode == "A":
        q_ref, k_ref, v_ref, alam_ref, subln_ref, o_ref = refs
    elif mode == "B":
        q_ref, k_ref, o_ref = refs
        v_ref = k_ref
    else:
        q_ref, k_ref, v_ref, lft_ref, o_ref, fcum_ref = refs
    i = pl.program_id(1)
    tk = tq
    split = mode != "B"
    rows = 2 * tq if split else tq

    if mode == "C":
        @pl.when(i == 0)
        def _():
            blk = 256
            r = lax.broadcasted_iota(jnp.int32, (blk, blk), 0)
            c = lax.broadcasted_iota(jnp.int32, (blk, blk), 1)
            upper = (r <= c).astype(F32)
            carry = jnp.zeros((C_HEADS, 1), F32)
            for b in range(seq // blk):
                part = lft_ref[:, b * blk:(b + 1) * blk]
                fcum_ref[:, b * blk:(b + 1) * blk] = carry + jnp.dot(
                    part, upper, preferred_element_type=F32, precision=HIGHEST)
                carry = carry + jnp.sum(part, axis=1, keepdims=True)

    if mode == "A":
        lam = _lambda(alam_ref, lam_init)
    n_tiles = B_HEADS if mode == "B" else 4
    for j in range(n_tiles):
        if split:
            qt = q_ref[:, j * LANES:(j + 1) * LANES]
            lane = lax.broadcasted_iota(jnp.int32, qt.shape, 1)
            zero = jnp.zeros_like(qt)
            qrows = jnp.concatenate([jnp.where(lane < 64, qt, zero), jnp.where(lane >= 64, qt, zero)], axis=0)
        else:
            qrows = q_ref[:, j * 256:(j + 1) * 256]
        if mode == "C":
            fdiag = fcum_ref[:, pl.ds(pl.multiple_of(i * tq, tq), tq)]
            fref = fdiag[:, tq - 1:tq]

        def step(jk, carry, masked, qrows=qrows, j=j):
            m, l, acc = carry
            off = pl.multiple_of(jk * tk, tk)
            k = k_ref[pl.ds(off, tk), :]
            s = lax.dot_general(qrows, k, _NT, preferred_element_type=F32)
            if mode == "B":
                s = s * SCALE_B
            if mode == "C":
                fb = fref - fcum_ref[:, pl.ds(off, tk)]
                s = s + jnp.concatenate([jnp.broadcast_to(fb[j:j + 1], (tq, tk)),
                                         jnp.broadcast_to(fb[4 + j:5 + j], (tq, tk))], axis=0)
            if masked:
                rp = lax.broadcasted_iota(jnp.int32, (rows, tk), 0)
                if split:
                    rp = jnp.where(rp >= tq, rp - tq, rp)
                cp = lax.broadcasted_iota(jnp.int32, (rows, tk), 1)
                s = jnp.where(rp >= cp, s, -jnp.inf)
            m_new = jnp.maximum(m, jnp.max(s, axis=1, keepdims=True))
            alpha = jnp.exp(m - m_new)
            p = jnp.exp(s - m_new)
            l = alpha * l + jnp.sum(p, axis=1, keepdims=True)
            v = v_ref[pl.ds(off, tk), 0:LANES]
            acc = alpha * acc + jnp.dot(p.astype(BF16), v, preferred_element_type=F32)
            return m_new, l, acc

        init = (jnp.full((rows, 1), -jnp.inf, F32), jnp.zeros((rows, 1), F32), jnp.zeros((rows, LANES), F32))
        carry = lax.fori_loop(0, i, lambda jk, c: step(jk, c, False), init)
        m, l, acc = step(i, carry, True)
        o = acc / l
        if mode == "A":
            res = _diff_finish(o[0:tq], o[tq:2 * tq], lam, subln_ref[...], lam_init)
        elif mode == "C":
            lane_o = lax.broadcasted_iota(jnp.int32, (tq, LANES), 1)
            res = jnp.where(lane_o < 64, o[0:tq], o[tq:2 * tq])
        else:
            res = o
        o_ref[:, j * LANES:(j + 1) * LANES] = res.astype(BF16)


def _flash_call(mode, q, k, v, extra, batch, seq, tq, lam_init=0.0):
    n = q.shape[0]
    nq = seq // tq
    qspec = pl.BlockSpec((tq, q.shape[1]), lambda b, i: (b * nq + i, 0))
    kspec = pl.BlockSpec((seq, k.shape[1]), lambda b, i: (b, 0))
    in_specs, args, scratch = [qspec, kspec], [q, k], []
    if mode != "B":
        in_specs.append(pl.BlockSpec((seq, v.shape[1]), lambda b, i: (b, 0)))
        args.append(v)
    if mode == "A":
        alam, subln = extra
        in_specs += [pl.BlockSpec(alam.shape, lambda b, i: (0, 0)), pl.BlockSpec(subln.shape, lambda b, i: (0, 0))]
        args += [alam, subln]
    if mode == "C":
        (lft,) = extra
        in_specs.append(pl.BlockSpec((C_HEADS, seq), lambda b, i: (0, b)))
        args.append(lft)
        scratch.append(pltpu.VMEM((C_HEADS, seq), F32))
    width = 1024 if mode == "B" else 512
    return pl.pallas_call(
        functools.partial(_flash_kernel, mode=mode, tq=tq, seq=seq, lam_init=lam_init),
        grid=(batch, nq),
        in_specs=in_specs,
        out_specs=pl.BlockSpec((tq, width), lambda b, i: (b * nq + i, 0)),
        out_shape=jax.ShapeDtypeStruct((n, width), BF16),
        scratch_shapes=scratch,
        compiler_params=_cparams(("arbitrary", "arbitrary")),
        name="prefill_attn_" + mode,
    )(*args)


NEW_PAD = 16


def _decode_kernel(pt_ref, qa_ref, qbl_ref, qbr_ref, qc_ref,
                   akn_ref, avn_ref, ckvn_ref, krn_ref, ckn_ref, cvn_ref, lfnt_ref,
                   alam_ref, subln_ref,
                   akt_hbm, av_hbm, ckv_hbm, krt_hbm, ckt_hbm, cvt_hbm, lft_hbm,
                   oa_ref, ob_ref, oc_ref,
                   akt_buf, av_buf, ckv_buf, krt_buf, ckt_buf, cvt_buf, lft_buf, sems,
                   *, layer, ppc, n_chunks, n_new, lam_init):
    s_idx = pl.program_id(0)
    n_seq = pl.num_programs(0)
    caches = (akt_hbm, av_hbm, ckv_hbm, krt_hbm, ckt_hbm, cvt_hbm, lft_hbm)
    bufs = (akt_buf, av_buf, ckv_buf, krt_buf, ckt_buf, cvt_buf, lft_buf)

    def copies(seq, chunk, slot):
        cps = []
        for p in range(ppc):
            page = pt_ref[seq, chunk * ppc + p]
            for a in range(len(caches)):
                cps.append(pltpu.make_async_copy(caches[a].at[layer, page], bufs[a].at[slot, p], sems.at[a, slot]))
        return cps

    def start(seq, chunk, slot):
        for cp in copies(seq, chunk, slot):
            cp.start()

    def wait(seq, chunk, slot):
        for cp in copies(seq, chunk, slot):
            cp.wait()

    @pl.when(s_idx == 0)
    def _():
        start(0, n_chunks - 1, 0)

    qa = qa_ref[...]
    qbl = qbl_ref[...]
    qbr = qbr_ref[...]
    qc = qc_ref[...]
    rows = qa.shape[0]

    def update(state, s_list, v_fn):
        m, l, acc = state
        cm = s_list[0].max(axis=1, keepdims=True)
        for s in s_list[1:]:
            cm = jnp.maximum(cm, s.max(axis=1, keepdims=True))
        m_new = jnp.maximum(m, cm)
        alpha = jnp.exp(m - m_new)
        l = alpha * l
        acc = alpha * acc
        for idx, s in enumerate(s_list):
            p = jnp.exp(s - m_new)
            l = l + jnp.sum(p, axis=1, keepdims=True)
            acc = acc + v_fn(idx, p.astype(BF16))
        return m_new, l, acc

    r_i = lax.broadcasted_iota(jnp.int32, (LANES, LANES), 0)
    c_i = lax.broadcasted_iota(jnp.int32, (LANES, LANES), 1)
    later = (r_i > c_i).astype(F32)

    def chunk_body(c, carry):
        st_a, st_b, st_c, suffix = carry
        g = s_idx * n_chunks + c
        slot = g % 2
        chunk = n_chunks - 1 - c

        @pl.when(c + 1 < n_chunks)
        def _():
            start(s_idx, chunk - 1, 1 - slot)

        @pl.when(jnp.logical_and(c + 1 == n_chunks, s_idx + 1 < n_seq))
        def _():
            start(s_idx + 1, n_chunks - 1, 1 - slot)

        wait(s_idx, chunk, slot)

        akt = [akt_buf[slot, p].astype(BF16) for p in range(ppc)]
        avs = [av_buf[slot, p].astype(BF16) for p in range(ppc)]
        s_a = [jnp.dot(qa, akt[p], preferred_element_type=F32) for p in range(ppc)]
        st_a = update(st_a, s_a, lambda p, pr: jnp.dot(pr, avs[p], preferred_element_type=F32))

        ckv = [ckv_buf[slot, p].astype(BF16) for p in range(ppc)]
        krt = [krt_buf[slot, p].astype(BF16) for p in range(ppc)]
        s_b = [(lax.dot_general(qbl, ckv[p], _NT, preferred_element_type=F32)
                + jnp.dot(qbr, krt[p], preferred_element_type=F32)) * SCALE_B for p in range(ppc)]
        st_b = update(st_b, s_b, lambda p, pr: jnp.dot(pr, ckv[p], preferred_element_type=F32))

        ckt = [ckt_buf[slot, p].astype(BF16) for p in range(ppc)]
        cvt = [cvt_buf[slot, p].astype(BF16) for p in range(ppc)]
        s_c = [None] * ppc
        for p in reversed(range(ppc)):
            lf = lft_buf[slot, p]
            bias = suffix + jnp.dot(lf, later, preferred_element_type=F32, precision=HIGHEST)
            suffix = suffix + jnp.sum(lf, axis=1, keepdims=True)
            s_c[p] = jnp.dot(qc, ckt[p], preferred_element_type=F32) + jnp.concatenate([bias] * n_new, axis=0)
        st_c = update(st_c, s_c, lambda p, pr: lax.dot_general(pr, cvt[p], _NT, preferred_element_type=F32))
        return st_a, st_b, st_c, suffix

    def init_state():
        return (jnp.full((rows, 1), -jnp.inf, F32), jnp.zeros((rows, 1), F32), jnp.zeros((rows, LANES), F32))

    row = lax.broadcasted_iota(jnp.int32, (rows, NEW_PAD), 0)
    col = lax.broadcasted_iota(jnp.int32, (rows, NEW_PAD), 1)
    tok_a = (row % (rows // 2)) // A_HEADS
    tok_bc = row // B_HEADS

    def masked(s, tok):
        return jnp.where(tok >= col, s, -jnp.inf)

    akn = akn_ref[...].astype(BF16)
    s = masked(lax.dot_general(qa, akn, _NT, preferred_element_type=F32), tok_a)
    avn = avn_ref[...].astype(BF16)
    st_a = update(init_state(), [s], lambda p, pr: jnp.dot(pr, avn, preferred_element_type=F32))

    ckvn = ckvn_ref[...].astype(BF16)
    s = (lax.dot_general(qbl, ckvn, _NT, preferred_element_type=F32)
         + lax.dot_general(qbr, krn_ref[...].astype(BF16), _NT, preferred_element_type=F32)) * SCALE_B
    st_b = update(init_state(), [masked(s, tok_bc)], lambda p, pr: jnp.dot(pr, ckvn, preferred_element_type=F32))

    r_n = lax.broadcasted_iota(jnp.int32, (NEW_PAD, NEW_PAD), 0)
    c_n = lax.broadcasted_iota(jnp.int32, (NEW_PAD, NEW_PAD), 1)
    upto = (r_n <= c_n).astype(F32)
    cum_new = jnp.dot(lfnt_ref[...], upto, preferred_element_type=F32, precision=HIGHEST)
    s = lax.dot_general(qc, ckn_ref[...].astype(BF16), _NT, preferred_element_type=F32)
    s = s - jnp.concatenate([cum_new] * n_new, axis=0)
    cvn = cvn_ref[...].astype(BF16)
    st_c = update(init_state(), [masked(s, tok_bc)], lambda p, pr: jnp.dot(pr, cvn, preferred_element_type=F32))

    st_a, st_b, st_c, _ = lax.fori_loop(
        0, n_chunks, chunk_body, (st_a, st_b, st_c, jnp.zeros((C_HEADS, 1), F32)))

    o_a = st_a[2] / st_a[1]
    half = rows // 2
    lam = _lambda(alam_ref, lam_init)
    oa_ref[...] = _diff_finish(o_a[0:half], o_a[half:rows], lam, subln_ref[...], lam_init)
    ob_ref[...] = st_b[2] / st_b[1]
    oc_ref[...] = st_c[2] / st_c[1]


def _decode_call(layer, page_table, qa, qbl, qbr, qc, news, alam, subln, caches, lam_init, ppc=8):
    n_seq, n_pages = page_table.shape
    n_chunks = n_pages // ppc
    rows = qa.shape[1]
    n_new = rows // B_HEADS

    def seq_spec(arr):
        return pl.BlockSpec((None,) + arr.shape[1:], lambda s, pt: (s,) + (0,) * (arr.ndim - 1))

    def const_spec(arr):
        return pl.BlockSpec(arr.shape, lambda s, pt: (0,) * arr.ndim)

    seq_args = [qa, qbl, qbr, qc] + list(news)
    in_specs = [seq_spec(a) for a in seq_args] + [const_spec(alam), const_spec(subln)]
    in_specs += [pl.BlockSpec(memory_space=pl.ANY)] * len(caches)
    scratch = [pltpu.VMEM((2, ppc) + c.shape[2:], F32) for c in caches]
    scratch.append(pltpu.SemaphoreType.DMA((len(caches), 2)))
    out_shape = [jax.ShapeDtypeStruct((n_seq, rows // 2, LANES), F32),
                 jax.ShapeDtypeStruct((n_seq, rows, LANES), F32),
                 jax.ShapeDtypeStruct((n_seq, rows, LANES), F32)]
    out_specs = [pl.BlockSpec((None, rows // 2, LANES), lambda s, pt: (s, 0, 0)),
                 pl.BlockSpec((None, rows, LANES), lambda s, pt: (s, 0, 0)),
                 pl.BlockSpec((None, rows, LANES), lambda s, pt: (s, 0, 0))]
    return pl.pallas_call(
        functools.partial(_decode_kernel, layer=layer, ppc=ppc, n_chunks=n_chunks, n_new=n_new,
                          lam_init=lam_init),
        grid_spec=pltpu.PrefetchScalarGridSpec(
            num_scalar_prefetch=1, grid=(n_seq,), in_specs=in_specs, out_specs=out_specs,
            scratch_shapes=scratch),
        out_shape=out_shape,
        compiler_params=_cparams(("arbitrary",)),
        name="decode_attn",
    )(page_table, *seq_args, alam, subln, *caches)


def _outproj_kernel(x_ref, oa_ref, ol_ref, oc_ref, gl_ref, g1_ref, sh2_ref, sc2_ref, gffn_ref,
                    woa_ref, wuv_ref, wob_ref, woc_ref, wout_ref, wrt_ref,
                    x1_o, h2_o, lgt_o):
    ya = jnp.dot(oa_ref[...], woa_ref[...], preferred_element_type=F32)
    ob = jnp.dot(ol_ref[...], wuv_ref[...], preferred_element_type=F32).astype(BF16)
    yb = jnp.dot(ob, wob_ref[...], preferred_element_type=F32)
    yc = jnp.dot(oc_ref[...], woc_ref[...], preferred_element_type=F32)
    merged = gl_ref[:, 0:1024] * ya + gl_ref[:, 1024:2048] * yb + gl_ref[:, 2048:3072] * yc
    x1 = x_ref[...] + g1_ref[...] * jnp.dot(merged.astype(BF16), wout_ref[...], preferred_element_type=F32)
    x1_o[...] = x1
    h2 = _rms(x1) * gffn_ref[...] * (1.0 + sc2_ref[...]) + sh2_ref[...]
    h2b = h2.astype(BF16)
    h2_o[...] = h2b
    lgt_o[...] = lax.dot_general(wrt_ref[...], h2b, _NT, preferred_element_type=F32)


def _outproj_call(x, oa, ol, oc, w_oc, gl, g1, sh2, sc2, lw, tm, tiles_per_seq):
    n, d = x.shape
    tps = tiles_per_seq
    in_specs = [_row_spec(tm, d), _row_spec(tm, 512), _row_spec(tm, 1024), _row_spec(tm, 512), _row_spec(tm, 3072),
                _mod_spec(g1, tm, tps), _mod_spec(sh2, tm, tps), _mod_spec(sc2, tm, tps), _full_spec((1, d)),
                _full_spec(lw["w_oa"].shape), _full_spec(lw["w_uv"].shape), _full_spec(lw["w_ob"].shape),
                _full_spec(w_oc.shape), _full_spec(lw["w_out"].shape), _full_spec(lw["w_rt"].shape)]
    return pl.pallas_call(
        _outproj_kernel,
        grid=(n // tm,),
        in_specs=in_specs,
        out_specs=[_row_spec(tm, d), _row_spec(tm, d), pl.BlockSpec((N_EXPERTS, tm), lambda i: (0, i))],
        out_shape=[jax.ShapeDtypeStruct((n, d), F32), jax.ShapeDtypeStruct((n, d), BF16),
                   jax.ShapeDtypeStruct((N_EXPERTS, n), F32)],
        compiler_params=_cparams(("arbitrary",)),
        name="out_proj",
    )(x, oa, ol, oc, gl, g1, sh2, sc2, lw["g_ffn"], lw["w_oa"], lw["w_uv"], lw["w_ob"], w_oc,
      lw["w_out"], lw["w_rt"])


def _take_top(x, iota, count, size):
    chosen = jnp.zeros(x.shape, F32)
    for _ in range(count):
        m = jnp.max(x, axis=0, keepdims=True)
        first = jnp.min(jnp.where(x == m, iota, size), axis=0, keepdims=True)
        hit = iota == first
        chosen = jnp.where(hit, 1.0, chosen)
        x = jnp.where(hit, -jnp.inf, x)
    return chosen


def _router_kernel(lgt_ref, eb_ref, comb_o):
    scores = _sigmoid(lgt_ref[...])
    biased = scores + eb_ref[...]
    tn = scores.shape[1]
    gsz = N_EXPERTS // N_GROUPS
    iota_g = lax.broadcasted_iota(jnp.int32, (gsz, tn), 0)
    gscore = []
    for g in range(N_GROUPS):
        xg = biased[g * gsz:(g + 1) * gsz]
        m1 = jnp.max(xg, axis=0, keepdims=True)
        first = jnp.min(jnp.where(xg == m1, iota_g, gsz), axis=0, keepdims=True)
        m2 = jnp.max(jnp.where(iota_g == first, -jnp.inf, xg), axis=0, keepdims=True)
        gscore.append(m1 + m2)
    gscore = jnp.concatenate(gscore, axis=0)
    iota_n = lax.broadcasted_iota(jnp.int32, (N_GROUPS, tn), 0)
    gsel = _take_top(gscore, iota_n, TOPK_GROUPS, N_GROUPS)
    cand = jnp.concatenate(
        [jnp.where(gsel[g:g + 1] > 0.0, biased[g * gsz:(g + 1) * gsz], -jnp.inf) for g in range(N_GROUPS)], axis=0)
    iota_e = lax.broadcasted_iota(jnp.int32, (N_EXPERTS, tn), 0)
    chosen = _take_top(cand, iota_e, TOP_K, N_EXPERTS)
    w = jnp.where(chosen > 0.0, scores, 0.0)
    comb = w / jnp.sum(w, axis=0, keepdims=True) * ROUTED_SCALE
    extra = (lax.broadcasted_iota(jnp.int32, (N_EXPERTS, tn), 0) == 0).astype(F32)
    comb_o[...] = jnp.concatenate([comb, extra], axis=0).T


def _router_call(lgt, e_bias, tn):
    n = lgt.shape[1]
    return pl.pallas_call(
        _router_kernel,
        grid=(n // tn,),
        in_specs=[pl.BlockSpec((N_EXPERTS, tn), lambda i: (0, i)), _full_spec((N_EXPERTS, 1))],
        out_specs=pl.BlockSpec((tn, 2 * N_EXPERTS), lambda i: (i, 0)),
        out_shape=jax.ShapeDtypeStruct((n, 2 * N_EXPERTS), F32),
        compiler_params=_cparams(("arbitrary",)),
        name="router_topk",
    )(lgt, e_bias)


def _moe_kernel(h_ref, comb_ref, x1_ref, g2_ref, gfin_ref, wg_ref, wu_ref, wd_ref, o_ref, acc_ref,
                *, eb, final_norm):
    k = pl.program_id(1)

    @pl.when(k == 0)
    def _():
        acc_ref[...] = jnp.zeros_like(acc_ref)

    hb = h_ref[...]
    comb = comb_ref[...]
    lane = lax.broadcasted_iota(jnp.int32, comb.shape, 1)
    for j in range(eb):
        e = k * eb + j
        cw = jnp.sum(jnp.where(lane == e, comb, 0.0), axis=1, keepdims=True)
        hg = jnp.dot(hb, wg_ref[j], preferred_element_type=F32)
        hu = jnp.dot(hb, wu_ref[j], preferred_element_type=F32)
        act = (_silu(hg) * hu * cw).astype(BF16)
        acc_ref[...] += jnp.dot(act, wd_ref[j], preferred_element_type=F32)

    @pl.when(k == pl.num_programs(1) - 1)
    def _():
        y = x1_ref[...] + g2_ref[...] * acc_ref[...]
        if final_norm:
            y = _rms(y) * gfin_ref[...]
        o_ref[...] = y


def _moe_call(h2, comb, x1, g2, g_final, lw, tm, tiles_per_seq, final_norm, eb=5):
    n, d = x1.shape
    n_e = lw["w_gate"].shape[0]
    tps = tiles_per_seq
    if g2.ndim == 3:
        g2_spec = pl.BlockSpec((None, 1, d), lambda i, k: (i // tps, 0, 0))
    else:
        g2_spec = pl.BlockSpec((tm, d), lambda i, k: (i, 0))
    f = lw["w_gate"].shape[2]
    return pl.pallas_call(
        functools.partial(_moe_kernel, eb=eb, final_norm=final_norm),
        grid=(n // tm, n_e // eb),
        in_specs=[pl.BlockSpec((tm, d), lambda i, k: (i, 0)),
                  pl.BlockSpec((tm, comb.shape[1]), lambda i, k: (i, 0)),
                  pl.BlockSpec((tm, d), lambda i, k: (i, 0)),
                  g2_spec,
                  pl.BlockSpec((1, d), lambda i, k: (0, 0)),
                  pl.BlockSpec((eb, d, f), lambda i, k: (k, 0, 0)),
                  pl.BlockSpec((eb, d, f), lambda i, k: (k, 0, 0)),
                  pl.BlockSpec((eb, f, d), lambda i, k: (k, 0, 0))],
        out_specs=pl.BlockSpec((tm, d), lambda i, k: (i, 0)),
        out_shape=jax.ShapeDtypeStruct((n, d), F32),
        scratch_shapes=[pltpu.VMEM((tm, d), F32)],
        compiler_params=_cparams(("arbitrary", "arbitrary")),
        name="moe_experts",
    )(h2, comb, x1, g2, g_final, lw["w_gate"], lw["w_up"], lw["w_down"])


def _rope_table(positions, head_dim, period_lanes):
    half = head_dim // 2
    inv = np.float32(ROPE_THETA) ** (-np.arange(half, dtype=np.float32) / np.float32(half))
    lane = np.arange(LANES)
    active = lane < period_lanes
    within = lane % head_dim
    ang = (positions.astype(np.float32)[:, None] * inv[within % half][None, :]).astype(np.float64)
    cos = np.where(active[None, :], np.cos(ang), 1.0)
    sin = np.where(active[None, :], np.sin(ang), 0.0)
    s_lo = np.where((within < half)[None, :], -sin, 0.0)
    s_hi = np.where((within >= half)[None, :], sin, 0.0)
    return jnp.asarray(np.concatenate([cos, s_lo, s_hi], axis=1), F32)


def _layer_weights(l, w_in, g_mix, g_bq, w_uq, w_uk, w_uv, g_bkv, b_f, w_oa, w_ob, w_oc, w_out, g_ffn,
                   w_router, e_bias, w_gate, w_up, w_down, ws_gate, ws_up, ws_down):
    d = w_in.shape[1]
    wi = w_in[l]
    cols = []
    for h in range(A_HEADS):
        cols += [wi[:, h * 64:(h + 1) * 64], wi[:, 256 + h * 64:256 + (h + 1) * 64]]
    cols += [wi[:, 512:640], wi[:, 640:768], wi[:, 768:1024], wi[:, 1024:1152]]
    for j in range(4):
        cols += [wi[:, 1184 + j * 64:1184 + (j + 1) * 64], wi[:, 1184 + (4 + j) * 64:1184 + (5 + j) * 64]]
    cols += [wi[:, 1696:1824], wi[:, 1824:1952]]
    cols += [wi[:, 1152:1184], wi[:, 1952:1960], jnp.zeros((d, LANES - 40), wi.dtype)]
    cols += [wi[:, 1960:5032]]
    w_in_p = jnp.concatenate(cols, axis=1).astype(BF16)

    uq = w_uq[l]
    zpad = jnp.zeros((B_Q_LORA, B_HEADS, LANES - B_NOPE_DIM), uq.dtype)
    w_uqn = jnp.concatenate([uq[:, :, :B_NOPE_DIM], zpad], axis=2).reshape(B_Q_LORA, B_HEADS * LANES)
    zpad = jnp.zeros((B_Q_LORA, B_HEADS, LANES - B_ROPE_DIM), uq.dtype)
    w_uqr = jnp.concatenate([uq[:, :, B_NOPE_DIM:], zpad], axis=2).reshape(B_Q_LORA, B_HEADS * LANES)
    uk = jnp.transpose(w_uk[l], (1, 2, 0))
    uk = jnp.concatenate([uk, jnp.zeros((B_HEADS, LANES - B_NOPE_DIM, B_KV_LORA), uk.dtype)], axis=1)
    uv = w_uv[l]
    eye = jnp.eye(B_HEADS, dtype=uv.dtype)
    w_uv_bd = jnp.einsum("chv,hg->hcgv", uv, eye).reshape(B_HEADS * B_KV_LORA, B_HEADS * B_V_DIM)
    oc = w_oc[l].reshape(C_HEADS, C_HEAD_DIM, d)
    w_oc_p = jnp.concatenate([oc[h] for j in range(4) for h in (j, 4 + j)], axis=0)

    return {
        "w_in": w_in_p,
        "w_cft": jnp.transpose(wi[:, 1952:1960]).astype(BF16),
        "g_mix": g_mix[l][None, :], "b_f": b_f[l][None, :], "b_ft": b_f[l][:, None],
        "g_bq": g_bq[l][None, :], "w_uqn": w_uqn.astype(BF16), "w_uqr": w_uqr.astype(BF16),
        "w_uk": uk.astype(BF16), "g_bkv": g_bkv[l][None, :],
        "w_oa": w_oa[l].astype(BF16), "w_uv": w_uv_bd.astype(BF16), "w_ob": w_ob[l].astype(BF16),
        "w_oc": w_oc_p.astype(BF16), "w_oc_hm": w_oc[l].astype(BF16), "w_out": w_out[l].astype(BF16),
        "g_ffn": g_ffn[l][None, :],
        "w_rt": jnp.transpose(w_router[l]).astype(BF16), "e_bias": e_bias[l][:, None],
        "w_gate": jnp.concatenate([w_gate[l], ws_gate[l][None]], axis=0).astype(BF16),
        "w_up": jnp.concatenate([w_up[l], ws_up[l][None]], axis=0).astype(BF16),
        "w_down": jnp.concatenate([w_down[l], ws_down[l][None]], axis=0).astype(BF16),
    }


def _pad_rows(a, rows):
    return jnp.concatenate([a, jnp.zeros((a.shape[0], rows - a.shape[1]) + a.shape[2:], a.dtype)], axis=1)


def kernel(x_prompt, x_sample, cache_a_k, cache_a_v, cache_b_ckv, cache_b_krope, cache_c_k, cache_c_v,
           cache_c_logf, page_table, c_prompt, c_sample, w_ada, b_ada, g_mix, w_in, a_lambda, a_subln, g_bq,
           w_uq, w_uk, w_uv, g_bkv, b_f, w_oa, w_ob, w_oc, w_out, g_ffn, w_router, e_bias, w_gate, w_up,
           w_down, ws_gate, ws_up, ws_down, g_final):
    batch, seq, d = x_prompt.shape
    n_seq, n_new, _ = x_sample.shape
    depth, n_pool, page = cache_a_k.shape[:3]
    past_len = page_table.shape[1] * page
    n_p, n_s = batch * seq, n_seq * n_new

    tm_p, tq = 512, 256
    tiles_p = seq // tm_p

    akt = jnp.transpose(cache_a_k, (0, 1, 3, 4, 5, 2)).reshape(depth, n_pool, 128, page)
    av = cache_a_v.reshape(depth, n_pool, page, 128)
    krt = jnp.transpose(cache_b_krope, (0, 1, 3, 2))
    ckt = jnp.transpose(cache_c_k, (0, 1, 3, 4, 2)).reshape(depth, n_pool, 128, page)
    cvt = jnp.transpose(cache_c_v, (0, 1, 3, 4, 2)).reshape(depth, n_pool, 128, page)
    lft = jnp.transpose(cache_c_logf, (0, 1, 3, 2))
    caches = (akt, av, cache_b_ckv, krt, ckt, cvt, lft)

    pos_p = np.arange(seq)
    pos_s = np.tile(past_len + np.arange(n_new), n_seq)
    t64_p, t32_p = _rope_table(pos_p, A_HEAD_DIM, LANES), _rope_table(pos_p, B_ROPE_DIM, B_ROPE_DIM)
    t64_s, t32_s = _rope_table(pos_s, A_HEAD_DIM, LANES), _rope_table(pos_s, B_ROPE_DIM, B_ROPE_DIM)

    c_all = jnp.concatenate([c_prompt, c_sample], axis=0)
    mod = _mod_call(c_all, w_ada, b_ada)

    xp = x_prompt.reshape(n_p, d)
    xs = x_sample.reshape(n_s, d)
    g_fin = g_final[None, :]
    rows_p, rows_s = [], []
    for l in range(depth):
        lam_init = 0.8 - 0.6 * math.exp(-0.3 * l)
        lw = _layer_weights(l, w_in, g_mix, g_bq, w_uq, w_uk, w_uv, g_bkv, b_f, w_oa, w_ob, w_oc, w_out,
                            g_ffn, w_router, e_bias, w_gate, w_up, w_down, ws_gate, ws_up, ws_down)
        alam, subln = a_lambda[l], a_subln[l][None, :]
        mods = jnp.split(mod[l], 6, axis=-1)
        mp = [m[:batch][:, None, :] for m in mods]
        ms = [jnp.repeat(m[batch:], n_new, axis=0) for m in mods]
        final = l == depth - 1

        pr = _inproj_call(xp, mp[0], mp[1], lw, t64_p, t32_p, tm_p, tiles_p)
        oa = _flash_call("A", pr["aq"], pr["akb"], pr["avb"], (alam, subln), batch, seq, tq, lam_init)
        ol = _flash_call("B", pr["qb"], pr["kbb"], None, (), batch, seq, tq)
        oc = _flash_call("C", pr["cq"], pr["ckb"], pr["cvb"], (pr["lft"],), batch, seq, tq)
        x1, h2, lgt = _outproj_call(xp, oa, ol, oc, lw["w_oc"], pr["gl"], mp[2], mp[3], mp[4], lw, tm_p, tiles_p)
        comb = _router_call(lgt, lw["e_bias"], 512)
        xp = _moe_call(h2, comb, x1, mp[5], g_fin, lw, tm_p, tiles_p, final)
        rows_p.append((pr["ak"].reshape(batch, seq, 2, 1, 64), pr["av"].reshape(batch, seq, 1, 128),
                       pr["ckv"].reshape(batch, seq, 128), pr["kr"].reshape(batch, seq, 32),
                       pr["ck"].reshape(batch, seq, 2, 64), pr["cv"].reshape(batch, seq, 2, 64),
                       pr["lf"].reshape(batch, seq, 8)))

        sr = _inproj_call(xs, ms[0], ms[1], lw, t64_s, t32_s, n_s, None)
        aq = sr["aq"].reshape(n_seq, n_new, A_HEADS, 2, 64)
        z = jnp.zeros_like(aq[..., 0, :])
        qa = jnp.stack([jnp.concatenate([aq[..., 0, :], z], axis=-1),
                        jnp.concatenate([z, aq[..., 1, :]], axis=-1)], axis=1)
        qa = qa.reshape(n_seq, 2 * n_new * A_HEADS, 128)
        qb = sr["qb"].reshape(n_seq, n_new * B_HEADS, 256)
        qbl, qbr = qb[..., :128], qb[..., 128:128 + B_ROPE_DIM]
        cq = sr["cq"].reshape(n_seq, n_new, 4, 2, 64)
        z = jnp.zeros_like(cq[..., 0, :])
        qc = jnp.concatenate([jnp.concatenate([cq[..., 0, :], z], axis=-1),
                              jnp.concatenate([z, cq[..., 1, :]], axis=-1)], axis=2)
        qc = qc.reshape(n_seq, n_new * C_HEADS, 128)
        news = [_pad_rows(sr[k].reshape(n_seq, n_new, -1), NEW_PAD) for k in ("ak", "av", "ckv", "kr", "ck", "cv")]
        lfnt = jnp.transpose(_pad_rows(sr["lf"].reshape(n_seq, n_new, C_HEADS), NEW_PAD), (0, 2, 1))
        oa_s, ob_s, oc_s = _decode_call(l, page_table, qa, qbl, qbr, qc, news + [lfnt], alam, subln, caches,
                                        lam_init)
        oa_s = oa_s.reshape(n_s, A_HEADS * A_V_DIM).astype(BF16)
        ol_s = ob_s.reshape(n_s, B_HEADS * B_KV_LORA).astype(BF16)
        oc4 = oc_s.reshape(n_seq, n_new, C_HEADS, 2, 64)
        oc_s = jnp.concatenate([oc4[:, :, :4, 0, :], oc4[:, :, 4:, 1, :]], axis=2)
        oc_s = oc_s.reshape(n_s, C_HEADS * C_HEAD_DIM).astype(BF16)
        x1, h2, lgt = _outproj_call(xs, oa_s, ol_s, oc_s, lw["w_oc_hm"], sr["gl"], ms[2], ms[3], ms[4], lw, n_s,
                                    None)
        comb = _router_call(lgt, lw["e_bias"], n_s)
        xs = _moe_call(h2, comb, x1, ms[5], g_fin, lw, n_s, None, final)
        rows_s.append((sr["ak"].reshape(n_seq, n_new, 2, 1, 64), sr["av"].reshape(n_seq, n_new, 1, 128),
                       sr["ckv"].reshape(n_seq, n_new, 128), sr["kr"].reshape(n_seq, n_new, 32),
                       sr["ck"].reshape(n_seq, n_new, 2, 64), sr["cv"].reshape(n_seq, n_new, 2, 64),
                       sr["lf"].reshape(n_seq, n_new, 8)))

    outs_p = [jnp.stack(r) for r in zip(*rows_p)]
    outs_s = [jnp.stack(r) for r in zip(*rows_s)]
    return (xp.reshape(batch, seq, d), xs.reshape(n_seq, n_new, d), *outs_p, *outs_s)
```

```python
import functools
import math

import numpy as np
import jax
import jax.numpy as jnp
from jax import lax
from jax.experimental import pallas as pl
from jax.experimental.pallas import tpu as pltpu

F32 = jnp.float32
BF16 = jnp.bfloat16
HIGHEST = lax.Precision.HIGHEST

A_HEADS, A_HEAD_DIM, A_V_DIM = 4, 64, 128
B_HEADS, B_Q_LORA, B_KV_LORA, B_NOPE_DIM, B_ROPE_DIM, B_V_DIM = 8, 256, 128, 64, 32, 64
C_HEADS, C_KV_HEADS, C_HEAD_DIM = 8, 2, 64
N_EXPERTS, TOP_K, N_GROUPS, TOPK_GROUPS, EXPERT_DIM = 64, 8, 8, 4, 256
ROUTED_SCALE = 2.5
ROPE_THETA = 10000.0
SCALE_B = (B_NOPE_DIM + B_ROPE_DIM) ** -0.5
RMS_EPS = 1e-6

LANES = 128
VMEM_LIMIT = 56 * 1024 * 1024

_AQ, _AK, _AV, _BCQ, _BCKV, _CQ, _CK, _CV, _MISC, _GL = (
    0, 512, 640, 768, 1024, 1152, 1664, 1792, 1920, 2048)
_IN_COLS = 5120

_NT = (((1,), (1,)), ((), ()))


def _cparams(sem):
    return pltpu.CompilerParams(dimension_semantics=sem, vmem_limit_bytes=VMEM_LIMIT)


def _rms(x):
    return x * lax.rsqrt(jnp.mean(x * x, axis=-1, keepdims=True) + RMS_EPS)


def _log_sigmoid(x):
    return jnp.minimum(x, 0.0) - jnp.log1p(jnp.exp(-jnp.abs(x)))


def _sigmoid(x):
    return 1.0 / (1.0 + jnp.exp(-x))


def _silu(x):
    return x * _sigmoid(x)


def _mod_kernel(c_ref, w_ref, b_ref, o_ref):
    c = _silu(c_ref[...]).astype(BF16)
    o_ref[...] = jnp.dot(c, w_ref[...].astype(BF16), preferred_element_type=F32) + b_ref[...]


def _mod_call(c_all, w_ada, b_ada):
    depth, d, d6 = w_ada.shape
    nc = c_all.shape[0]
    tn = 1024
    return pl.pallas_call(
        _mod_kernel,
        grid=(depth, d6 // tn),
        in_specs=[pl.BlockSpec((nc, d), lambda l, j: (0, 0)),
                  pl.BlockSpec((None, d, tn), lambda l, j: (l, 0, j)),
                  pl.BlockSpec((None, 1, tn), lambda l, j: (l, 0, j))],
        out_specs=pl.BlockSpec((None, nc, tn), lambda l, j: (l, 0, j)),
        out_shape=jax.ShapeDtypeStruct((depth, nc, d6), F32),
        compiler_params=_cparams(("arbitrary", "arbitrary")),
        name="ada_mod",
    )(c_all, w_ada, b_ada.reshape(depth, 1, d6))


def _rope_tiles(z, tab_ref, half):
    cos = tab_ref[:, 0:LANES]
    s_lo = tab_ref[:, LANES:2 * LANES]
    s_hi = tab_ref[:, 2 * LANES:3 * LANES]
    outs = []
    for i in range(z.shape[1] // LANES):
        t = z[:, i * LANES:(i + 1) * LANES]
        outs.append(t * cos + pltpu.roll(t, LANES - half, 1) * s_lo + pltpu.roll(t, half, 1) * s_hi)
    return outs


def _inproj_kernel(x_ref, sh_ref, sc_ref, g_ref, w_ref, wcft_ref, bf_ref, bft_ref, t64_ref, t32_ref,
                   gbq_ref, wuqn_ref, wuqr_ref, wuk_ref, gbkv_ref,
                   aq_o, ak_o, av_o, qb_o, ckv_o, kr_o, cq_o, ck_o, cv_o, lf_o, lft_o, gl_o,
                   akb_o, avb_o, kbb_o, ckb_o, cvb_o):
    h = _rms(x_ref[...]) * g_ref[...] * (1.0 + sc_ref[...]) + sh_ref[...]
    hb = h.astype(BF16)

    def seg(lo, n):
        return jnp.dot(hb, w_ref[:, lo:lo + n], preferred_element_type=F32)

    aq = _rope_tiles(seg(_AQ, 512), t64_ref, A_HEAD_DIM // 2)
    for i, t in enumerate(aq):
        aq_o[:, i * LANES:(i + 1) * LANES] = (t * (A_HEAD_DIM ** -0.5)).astype(BF16)
    ak = _rope_tiles(seg(_AK, 128), t64_ref, A_HEAD_DIM // 2)[0]
    ak_o[...] = ak
    akb_o[...] = ak.astype(BF16)
    av = seg(_AV, 128)
    av_o[...] = av
    avb_o[...] = av.astype(BF16)

    qn = (_rms(seg(_BCQ, B_Q_LORA)) * gbq_ref[...]).astype(BF16)
    nope = jnp.dot(qn, wuqn_ref[...], preferred_element_type=F32).astype(BF16)
    qrope = _rope_tiles(jnp.dot(qn, wuqr_ref[...], preferred_element_type=F32), t32_ref, B_ROPE_DIM // 2)
    for hd in range(B_HEADS):
        qlat = jnp.dot(nope[:, hd * LANES:(hd + 1) * LANES], wuk_ref[hd], preferred_element_type=F32)
        qb_o[:, hd * 256:hd * 256 + LANES] = qlat.astype(BF16)
        qb_o[:, hd * 256 + LANES:(hd + 1) * 256] = qrope[hd].astype(BF16)
    ckv = _rms(seg(_BCKV, B_KV_LORA)) * gbkv_ref[...]
    ckv_o[...] = ckv
    misc = _rope_tiles(seg(_MISC, LANES), t32_ref, B_ROPE_DIM // 2)[0]
    kr_o[...] = misc[:, 0:B_ROPE_DIM]
    lane = lax.broadcasted_iota(jnp.int32, misc.shape, 1)
    kbb_o[:, 0:LANES] = ckv.astype(BF16)
    kbb_o[:, LANES:2 * LANES] = jnp.where(lane < B_ROPE_DIM, misc, 0.0).astype(BF16)

    cq = seg(_CQ, 512)
    cq_o[...] = (cq * (C_HEAD_DIM ** -0.5)).astype(BF16)
    ck = seg(_CK, 128)
    ck_o[...] = ck
    ckb_o[...] = ck.astype(BF16)
    cv = seg(_CV, 128)
    cv_o[...] = cv
    cvb_o[...] = cv.astype(BF16)
    lf_o[...] = _log_sigmoid(misc[:, B_ROPE_DIM:B_ROPE_DIM + C_HEADS] + bf_ref[...])
    cft = lax.dot_general(wcft_ref[...], hb, _NT, preferred_element_type=F32)
    lft_o[...] = _log_sigmoid(cft + bft_ref[...])

    for i in range(3):
        gl_o[:, i * 1024:(i + 1) * 1024] = _sigmoid(seg(_GL + i * 1024, 1024))


def _row_spec(tm, width):
    return pl.BlockSpec((tm, width), lambda i: (i, 0))


def _full_spec(shape):
    nd = len(shape)
    return pl.BlockSpec(shape, lambda i: (0,) * nd)


def _mod_spec(mod_arr, tm, tiles_per_seq):
    d = mod_arr.shape[-1]
    if mod_arr.ndim == 3:
        return pl.BlockSpec((None, 1, d), lambda i: (i // tiles_per_seq, 0, 0))
    return pl.BlockSpec((tm, d), lambda i: (i, 0))


def _tab_spec(tab, tm, tiles_per_seq):
    if tiles_per_seq is None:
        return pl.BlockSpec((tm, tab.shape[1]), lambda i: (i, 0))
    return pl.BlockSpec((tm, tab.shape[1]), lambda i: (i % tiles_per_seq, 0))


def _inproj_call(x, sh, sc, lw, t64, t32, tm, tiles_per_seq):
    n, d = x.shape
    outs = [("aq", 512, BF16), ("ak", 128, F32), ("av", 128, F32), ("qb", 2048, BF16), ("ckv", 128, F32),
            ("kr", 32, F32), ("cq", 512, BF16), ("ck", 128, F32), ("cv", 128, F32), ("lf", 8, F32),
            ("lft", None, F32), ("gl", 3072, F32), ("akb", 128, BF16), ("avb", 128, BF16),
            ("kbb", 256, BF16), ("ckb", 128, BF16), ("cvb", 128, BF16)]
    out_shape, out_specs = [], []
    for name, width, dt in outs:
        if name == "lft":
            out_shape.append(jax.ShapeDtypeStruct((C_HEADS, n), dt))
            out_specs.append(pl.BlockSpec((C_HEADS, tm), lambda i: (0, i)))
        else:
            out_shape.append(jax.ShapeDtypeStruct((n, width), dt))
            out_specs.append(_row_spec(tm, width))
    tps = tiles_per_seq
    in_specs = [_row_spec(tm, d), _mod_spec(sh, tm, tps), _mod_spec(sc, tm, tps), _full_spec((1, d)),
                _full_spec(lw["w_in"].shape), _full_spec(lw["w_cft"].shape), _full_spec((1, C_HEADS)),
                _full_spec((C_HEADS, 1)), _tab_spec(t64, tm, tps), _tab_spec(t32, tm, tps),
                _full_spec((1, B_Q_LORA)), _full_spec(lw["w_uqn"].shape), _full_spec(lw["w_uqr"].shape),
                _full_spec(lw["w_uk"].shape), _full_spec((1, B_KV_LORA))]
    res = pl.pallas_call(
        _inproj_kernel,
        grid=(n // tm,),
        in_specs=in_specs,
        out_specs=out_specs,
        out_shape=out_shape,
        compiler_params=_cparams(("arbitrary",)),
        name="in_proj",
    )(x, sh, sc, lw["g_mix"], lw["w_in"], lw["w_cft"], lw["b_f"], lw["b_ft"], t64, t32,
      lw["g_bq"], lw["w_uqn"], lw["w_uqr"], lw["w_uk"], lw["g_bkv"])
    return {name: r for (name, _, _), r in zip(outs, res)}


def _lambda(alam_ref, lam_init):
    a = alam_ref[...]
    s01 = jnp.sum(a[0:1] * a[1:2], axis=1, keepdims=True)
    s23 = jnp.sum(a[2:3] * a[3:4], axis=1, keepdims=True)
    return jnp.exp(s01) - jnp.exp(s23) + lam_init


def _diff_finish(o1, o2, lam, subln, lam_init):
    d = o1 - lam * o2
    return _rms(d) * subln * (1.0 - lam_init)


def _flash_kernel(*refs, mode, tq, seq, lam_init, group):
    if mode == "A":
        q_ref, k_ref, v_ref, alam_ref, subln_ref, o_ref = refs
    elif mode == "B":
        q_ref, k_ref, o_ref = refs
        v_ref = k_ref
    else:
        q_ref, k_ref, v_ref, lft_ref, o_ref, fcum_ref = refs
    i = pl.program_id(1)
    tk = tq
    split = mode != "B"

    if mode == "C":
        @pl.when(i == 0)
        def _():
            blk = 256
            r = lax.broadcasted_iota(jnp.int32, (blk, blk), 0)
            c = lax.broadcasted_iota(jnp.int32, (blk, blk), 1)
            upper = (r <= c).astype(F32)
            carry = jnp.zeros((C_HEADS, 1), F32)
            for b in range(seq // blk):
                part = lft_ref[:, b * blk:(b + 1) * blk]
                fcum_ref[:, b * blk:(b + 1) * blk] = carry + jnp.dot(
                    part, upper, preferred_element_type=F32, precision=HIGHEST)
                carry = carry + jnp.sum(part, axis=1, keepdims=True)

    if mode == "A":
        lam = _lambda(alam_ref, lam_init)
    if mode == "C":
        fdiag = fcum_ref[:, pl.ds(pl.multiple_of(i * tq, tq), tq)]
        fref = fdiag[:, tq - 1:tq]
    n_tiles = B_HEADS if mode == "B" else 4
    rows = 2 * tq if split else tq

    def q_rows(j):
        if not split:
            return q_ref[:, j * 256:(j + 1) * 256]
        qt = q_ref[:, j * LANES:(j + 1) * LANES]
        lane = lax.broadcasted_iota(jnp.int32, qt.shape, 1)
        zero = jnp.zeros_like(qt)
        return jnp.concatenate([jnp.where(lane < 64, qt, zero), jnp.where(lane >= 64, qt, zero)], axis=0)

    def step(jk, carry, masked, qrows, j):
        m, l, acc = carry
        off = pl.multiple_of(jk * tk, tk)
        k = k_ref[pl.ds(off, tk), :]
        s = lax.dot_general(qrows, k, _NT, preferred_element_type=F32)
        if mode == "B":
            s = s * SCALE_B
        if mode == "C":
            fb = fref - fcum_ref[:, pl.ds(off, tk)]
            s = s + jnp.concatenate([jnp.broadcast_to(fb[j:j + 1], (tq, tk)),
                                     jnp.broadcast_to(fb[4 + j:5 + j], (tq, tk))], axis=0)
        if masked:
            rp = lax.broadcasted_iota(jnp.int32, (rows, tk), 0) & (tq - 1)
            cp = lax.broadcasted_iota(jnp.int32, (rows, tk), 1)
            s = jnp.where(rp >= cp, s, -jnp.inf)
        m_new = jnp.maximum(m, jnp.max(s, axis=1, keepdims=True))
        alpha = jnp.exp(m - m_new)
        p = jnp.exp(s - m_new)
        l = alpha * l + jnp.sum(p, axis=1, keepdims=True)
        v = v_ref[pl.ds(off, tk), 0:LANES]
        acc = alpha * acc + jnp.dot(p.astype(BF16), v, preferred_element_type=F32)
        return m_new, l, acc

    for j0 in range(0, n_tiles, group):
        js = list(range(j0, j0 + group))
        qs = [q_rows(j) for j in js]
        init = tuple((jnp.full((rows, 1), -jnp.inf, F32), jnp.zeros((rows, 1), F32), jnp.zeros((rows, LANES), F32))
                     for _ in js)

        def body(jk, carry, qs=qs, js=js):
            return tuple(step(jk, c, False, q, j) for c, q, j in zip(carry, qs, js))

        carry = lax.fori_loop(0, i, body, init)
        for c, q, j in zip(carry, qs, js):
            m, l, acc = step(i, c, True, q, j)
            o = acc / l
            if mode == "A":
                res = _diff_finish(o[0:tq], o[tq:2 * tq], lam, subln_ref[...], lam_init)
            elif mode == "C":
                lane_o = lax.broadcasted_iota(jnp.int32, (tq, LANES), 1)
                res = jnp.where(lane_o < 64, o[0:tq], o[tq:2 * tq])
            else:
                res = o
            o_ref[:, j * LANES:(j + 1) * LANES] = res.astype(BF16)


def _flash_call(mode, q, k, v, extra, batch, seq, tq, lam_init=0.0, group=2):
    n = q.shape[0]
    nq = seq // tq
    qspec = pl.BlockSpec((tq, q.shape[1]), lambda b, i: (b * nq + i, 0))
    kspec = pl.BlockSpec((seq, k.shape[1]), lambda b, i: (b, 0))
    in_specs, args, scratch = [qspec, kspec], [q, k], []
    if mode != "B":
        in_specs.append(pl.BlockSpec((seq, v.shape[1]), lambda b, i: (b, 0)))
        args.append(v)
    if mode == "A":
        alam, subln = extra
        in_specs += [pl.BlockSpec(alam.shape, lambda b, i: (0, 0)), pl.BlockSpec(subln.shape, lambda b, i: (0, 0))]
        args += [alam, subln]
    if mode == "C":
        (lft,) = extra
        in_specs.append(pl.BlockSpec((C_HEADS, seq), lambda b, i: (0, b)))
        args.append(lft)
        scratch.append(pltpu.VMEM((C_HEADS, seq), F32))
    width = 1024 if mode == "B" else 512
    return pl.pallas_call(
        functools.partial(_flash_kernel, mode=mode, tq=tq, seq=seq, lam_init=lam_init, group=group),
        grid=(batch, nq),
        in_specs=in_specs,
        out_specs=pl.BlockSpec((tq, width), lambda b, i: (b * nq + i, 0)),
        out_shape=jax.ShapeDtypeStruct((n, width), BF16),
        scratch_shapes=scratch,
        compiler_params=_cparams(("arbitrary", "arbitrary")),
        name="prefill_attn_" + mode,
    )(*args)


NEW_PAD = 16


def _decode_kernel(pt_ref, qa_ref, qbl_ref, qbr_ref, qc_ref,
                   akn_ref, avn_ref, ckvn_ref, krn_ref, ckn_ref, cvn_ref, lfnt_ref,
                   alam_ref, subln_ref,
                   akt_hbm, av_hbm, ckv_hbm, krt_hbm, ckt_hbm, cvt_hbm, lft_hbm,
                   oa_ref, ob_ref, oc_ref,
                   akt_buf, av_buf, ckv_buf, krt_buf, ckt_buf, cvt_buf, lft_buf, sems,
                   *, layer, ppc, n_chunks, n_new, lam_init):
    s_idx = pl.program_id(0)
    n_seq = pl.num_programs(0)
    caches = (akt_hbm, av_hbm, ckv_hbm, krt_hbm, ckt_hbm, cvt_hbm, lft_hbm)
    bufs = (akt_buf, av_buf, ckv_buf, krt_buf, ckt_buf, cvt_buf, lft_buf)

    def copies(seq, chunk, slot):
        cps = []
        for p in range(ppc):
            page = pt_ref[seq, chunk * ppc + p]
            for a in range(len(caches)):
                cps.append(pltpu.make_async_copy(caches[a].at[layer, page], bufs[a].at[slot, p], sems.at[a, slot]))
        return cps

    def start(seq, chunk, slot):
        for cp in copies(seq, chunk, slot):
            cp.start()

    def wait(seq, chunk, slot):
        for cp in copies(seq, chunk, slot):
            cp.wait()

    @pl.when(s_idx == 0)
    def _():
        start(0, n_chunks - 1, 0)

    qa = qa_ref[...]
    qbl = qbl_ref[...]
    qbr = qbr_ref[...]
    qc = qc_ref[...]
    rows = qa.shape[0]

    def update(state, s_list, v_fn):
        m, l, acc = state
        cm = s_list[0].max(axis=1, keepdims=True)
        for s in s_list[1:]:
            cm = jnp.maximum(cm, s.max(axis=1, keepdims=True))
        m_new = jnp.maximum(m, cm)
        alpha = jnp.exp(m - m_new)
        l = alpha * l
        acc = alpha * acc
        for idx, s in enumerate(s_list):
            p = jnp.exp(s - m_new)
            l = l + jnp.sum(p, axis=1, keepdims=True)
            acc = acc + v_fn(idx, p.astype(BF16))
        return m_new, l, acc

    def soft(state, s):
        m, l, _ = state
        m_new = jnp.maximum(m, jnp.max(s, axis=1, keepdims=True))
        alpha = jnp.exp(m - m_new)
        p = jnp.exp(s - m_new)
        return m_new, alpha, alpha * l + jnp.sum(p, axis=1, keepdims=True), p.astype(BF16)

    r_i = lax.broadcasted_iota(jnp.int32, (LANES, LANES), 0)
    c_i = lax.broadcasted_iota(jnp.int32, (LANES, LANES), 1)
    later = (r_i > c_i).astype(BF16)

    def chunk_body(c, carry):
        st_a, st_b, st_c, suffix = carry
        g = s_idx * n_chunks + c
        slot = g % 2
        chunk = n_chunks - 1 - c

        @pl.when(c + 1 < n_chunks)
        def _():
            start(s_idx, chunk - 1, 1 - slot)

        @pl.when(jnp.logical_and(c + 1 == n_chunks, s_idx + 1 < n_seq))
        def _():
            start(s_idx + 1, n_chunks - 1, 1 - slot)

        wait(s_idx, chunk, slot)

        def keys_on_lanes(buf):
            return jnp.concatenate([buf[slot, p].astype(BF16) for p in range(ppc)], axis=1)

        def keys_on_rows(buf):
            return jnp.concatenate([buf[slot, p].astype(BF16) for p in range(ppc)], axis=0)

        akt = keys_on_lanes(akt_buf)
        avs = keys_on_rows(av_buf)
        s_a = jnp.dot(qa, akt, preferred_element_type=F32)

        ckv = keys_on_rows(ckv_buf)
        krt = keys_on_lanes(krt_buf)
        s_b = (lax.dot_general(qbl, ckv, _NT, preferred_element_type=F32)
               + jnp.dot(qbr, krt, preferred_element_type=F32)) * SCALE_B

        ckt = keys_on_lanes(ckt_buf)
        cvt = keys_on_lanes(cvt_buf)
        lf = lft_buf[slot].reshape(ppc * C_HEADS, LANES)
        hi = lf.astype(BF16)
        rem = lf - hi.astype(F32)
        mid = rem.astype(BF16)
        lo = (rem - mid.astype(F32)).astype(BF16)
        parts = jnp.dot(jnp.concatenate([hi, mid, lo], axis=0), later, preferred_element_type=F32)
        nr = ppc * C_HEADS
        within = parts[0:nr] + parts[nr:2 * nr] + parts[2 * nr:3 * nr]
        page_sum = jnp.sum(lf, axis=1, keepdims=True)
        bias = [None] * ppc
        for p in reversed(range(ppc)):
            bias[p] = within[p * C_HEADS:(p + 1) * C_HEADS] + suffix
            suffix = suffix + page_sum[p * C_HEADS:(p + 1) * C_HEADS]
        bias = jnp.concatenate(bias, axis=1)
        s_c = jnp.dot(qc, ckt, preferred_element_type=F32) + jnp.concatenate([bias] * n_new, axis=0)

        m_a, al_a, l_a, p_a = soft(st_a, s_a)
        m_b, al_b, l_b, p_b = soft(st_b, s_b)
        m_c, al_c, l_c, p_c = soft(st_c, s_c)
        st_a = (m_a, l_a, al_a * st_a[2] + jnp.dot(p_a, avs, preferred_element_type=F32))
        st_b = (m_b, l_b, al_b * st_b[2] + jnp.dot(p_b, ckv, preferred_element_type=F32))
        st_c = (m_c, l_c, al_c * st_c[2] + lax.dot_general(p_c, cvt, _NT, preferred_element_type=F32))
        return st_a, st_b, st_c, suffix

    def init_state():
        return (jnp.full((rows, 1), -jnp.inf, F32), jnp.zeros((rows, 1), F32), jnp.zeros((rows, LANES), F32))

    row = lax.broadcasted_iota(jnp.int32, (rows, NEW_PAD), 0)
    col = lax.broadcasted_iota(jnp.int32, (rows, NEW_PAD), 1)
    tok_a = (row % (rows // 2)) // A_HEADS
    tok_bc = row // B_HEADS

    def masked(s, tok):
        return jnp.where(tok >= col, s, -jnp.inf)

    akn = akn_ref[...].astype(BF16)
    s = masked(lax.dot_general(qa, akn, _NT, preferred_element_type=F32), tok_a)
    avn = avn_ref[...].astype(BF16)
    st_a = update(init_state(), [s], lambda p, pr: jnp.dot(pr, avn, preferred_element_type=F32))

    ckvn = ckvn_ref[...].astype(BF16)
    s = (lax.dot_general(qbl, ckvn, _NT, preferred_element_type=F32)
         + lax.dot_general(qbr, krn_ref[...].astype(BF16), _NT, preferred_element_type=F32)) * SCALE_B
    st_b = update(init_state(), [masked(s, tok_bc)], lambda p, pr: jnp.dot(pr, ckvn, preferred_element_type=F32))

    r_n = lax.broadcasted_iota(jnp.int32, (NEW_PAD, NEW_PAD), 0)
    c_n = lax.broadcasted_iota(jnp.int32, (NEW_PAD, NEW_PAD), 1)
    upto = (r_n <= c_n).astype(F32)
    cum_new = jnp.dot(lfnt_ref[...], upto, preferred_element_type=F32, precision=HIGHEST)
    s = lax.dot_general(qc, ckn_ref[...].astype(BF16), _NT, preferred_element_type=F32)
    s = s - jnp.concatenate([cum_new] * n_new, axis=0)
    cvn = cvn_ref[...].astype(BF16)
    st_c = update(init_state(), [masked(s, tok_bc)], lambda p, pr: jnp.dot(pr, cvn, preferred_element_type=F32))

    st_a, st_b, st_c, _ = lax.fori_loop(
        0, n_chunks, chunk_body, (st_a, st_b, st_c, jnp.zeros((C_HEADS, 1), F32)))

    o_a = st_a[2] / st_a[1]
    half = rows // 2
    lam = _lambda(alam_ref, lam_init)
    oa_ref[...] = _diff_finish(o_a[0:half], o_a[half:rows], lam, subln_ref[...], lam_init)
    ob_ref[...] = st_b[2] / st_b[1]
    oc_ref[...] = st_c[2] / st_c[1]


def _decode_call(layer, page_table, qa, qbl, qbr, qc, news, alam, subln, caches, lam_init, ppc=8):
    n_seq, n_pages = page_table.shape
    n_chunks = n_pages // ppc
    rows = qa.shape[1]
    n_new = rows // B_HEADS

    def seq_spec(arr):
        return pl.BlockSpec((None,) + arr.shape[1:], lambda s, pt: (s,) + (0,) * (arr.ndim - 1))

    def const_spec(arr):
        return pl.BlockSpec(arr.shape, lambda s, pt: (0,) * arr.ndim)

    seq_args = [qa, qbl, qbr, qc] + list(news)
    in_specs = [seq_spec(a) for a in seq_args] + [const_spec(alam), const_spec(subln)]
    in_specs += [pl.BlockSpec(memory_space=pl.ANY)] * len(caches)
    scratch = [pltpu.VMEM((2, ppc) + c.shape[2:], F32) for c in caches]
    scratch.append(pltpu.SemaphoreType.DMA((len(caches), 2)))
    out_shape = [jax.ShapeDtypeStruct((n_seq, rows // 2, LANES), F32),
                 jax.ShapeDtypeStruct((n_seq, rows, LANES), F32),
                 jax.ShapeDtypeStruct((n_seq, rows, LANES), F32)]
    out_specs = [pl.BlockSpec((None, rows // 2, LANES), lambda s, pt: (s, 0, 0)),
                 pl.BlockSpec((None, rows, LANES), lambda s, pt: (s, 0, 0)),
                 pl.BlockSpec((None, rows, LANES), lambda s, pt: (s, 0, 0))]
    return pl.pallas_call(
        functools.partial(_decode_kernel, layer=layer, ppc=ppc, n_chunks=n_chunks, n_new=n_new,
                          lam_init=lam_init),
        grid_spec=pltpu.PrefetchScalarGridSpec(
            num_scalar_prefetch=1, grid=(n_seq,), in_specs=in_specs, out_specs=out_specs,
            scratch_shapes=scratch),
        out_shape=out_shape,
        compiler_params=_cparams(("arbitrary",)),
        name="decode_attn",
    )(page_table, *seq_args, alam, subln, *caches)


def _outproj_kernel(x_ref, oa_ref, ol_ref, oc_ref, gl_ref, g1_ref, sh2_ref, sc2_ref, gffn_ref,
                    woa_ref, wuv_ref, wob_ref, woc_ref, wout_ref, wrt_ref,
                    x1_o, h2_o, lgt_o):
    ya = jnp.dot(oa_ref[...], woa_ref[...], preferred_element_type=F32)
    ob = jnp.dot(ol_ref[...], wuv_ref[...], preferred_element_type=F32).astype(BF16)
    yb = jnp.dot(ob, wob_ref[...], preferred_element_type=F32)
    yc = jnp.dot(oc_ref[...], woc_ref[...], preferred_element_type=F32)
    merged = gl_ref[:, 0:1024] * ya + gl_ref[:, 1024:2048] * yb + gl_ref[:, 2048:3072] * yc
    x1 = x_ref[...] + g1_ref[...] * jnp.dot(merged.astype(BF16), wout_ref[...], preferred_element_type=F32)
    x1_o[...] = x1
    h2 = _rms(x1) * gffn_ref[...] * (1.0 + sc2_ref[...]) + sh2_ref[...]
    h2b = h2.astype(BF16)
    h2_o[...] = h2b
    lgt_o[...] = lax.dot_general(wrt_ref[...], h2b, _NT, preferred_element_type=F32)


def _outproj_call(x, oa, ol, oc, w_oc, gl, g1, sh2, sc2, lw, tm, tiles_per_seq):
    n, d = x.shape
    tps = tiles_per_seq
    in_specs = [_row_spec(tm, d), _row_spec(tm, 512), _row_spec(tm, 1024), _row_spec(tm, 512), _row_spec(tm, 3072),
                _mod_spec(g1, tm, tps), _mod_spec(sh2, tm, tps), _mod_spec(sc2, tm, tps), _full_spec((1, d)),
                _full_spec(lw["w_oa"].shape), _full_spec(lw["w_uv"].shape), _full_spec(lw["w_ob"].shape),
                _full_spec(w_oc.shape), _full_spec(lw["w_out"].shape), _full_spec(lw["w_rt"].shape)]
    return pl.pallas_call(
        _outproj_kernel,
        grid=(n // tm,),
        in_specs=in_specs,
        out_specs=[_row_spec(tm, d), _row_spec(tm, d), pl.BlockSpec((N_EXPERTS, tm), lambda i: (0, i))],
        out_shape=[jax.ShapeDtypeStruct((n, d), F32), jax.ShapeDtypeStruct((n, d), BF16),
                   jax.ShapeDtypeStruct((N_EXPERTS, n), F32)],
        compiler_params=_cparams(("arbitrary",)),
        name="out_proj",
    )(x, oa, ol, oc, gl, g1, sh2, sc2, lw["g_ffn"], lw["w_oa"], lw["w_uv"], lw["w_ob"], w_oc,
      lw["w_out"], lw["w_rt"])


def _take_top(x, iota, count, size):
    chosen = jnp.zeros(x.shape, F32)
    for _ in range(count):
        m = jnp.max(x, axis=0, keepdims=True)
        first = jnp.min(jnp.where(x == m, iota, size), axis=0, keepdims=True)
        hit = iota == first
        chosen = jnp.where(hit, 1.0, chosen)
        x = jnp.where(hit, -jnp.inf, x)
    return chosen


def _router_kernel(lgt_ref, eb_ref, comb_o):
    scores = _sigmoid(lgt_ref[...])
    biased = scores + eb_ref[...]
    tn = scores.shape[1]
    gsz = N_EXPERTS // N_GROUPS
    iota_g = lax.broadcasted_iota(jnp.int32, (gsz, tn), 0)
    gscore = []
    for g in range(N_GROUPS):
        xg = biased[g * gsz:(g + 1) * gsz]
        m1 = jnp.max(xg, axis=0, keepdims=True)
        first = jnp.min(jnp.where(xg == m1, iota_g, gsz), axis=0, keepdims=True)
        m2 = jnp.max(jnp.where(iota_g == first, -jnp.inf, xg), axis=0, keepdims=True)
        gscore.append(m1 + m2)
    gscore = jnp.concatenate(gscore, axis=0)
    iota_n = lax.broadcasted_iota(jnp.int32, (N_GROUPS, tn), 0)
    gsel = _take_top(gscore, iota_n, TOPK_GROUPS, N_GROUPS)
    cand = jnp.concatenate(
        [jnp.where(gsel[g:g + 1] > 0.0, biased[g * gsz:(g + 1) * gsz], -jnp.inf) for g in range(N_GROUPS)], axis=0)
    iota_e = lax.broadcasted_iota(jnp.int32, (N_EXPERTS, tn), 0)
    chosen = _take_top(cand, iota_e, TOP_K, N_EXPERTS)
    w = jnp.where(chosen > 0.0, scores, 0.0)
    comb = w / jnp.sum(w, axis=0, keepdims=True) * ROUTED_SCALE
    extra = (lax.broadcasted_iota(jnp.int32, (N_EXPERTS, tn), 0) == 0).astype(F32)
    comb_o[...] = jnp.concatenate([comb, extra], axis=0).T


def _router_call(lgt, e_bias, tn):
    n = lgt.shape[1]
    return pl.pallas_call(
        _router_kernel,
        grid=(n // tn,),
        in_specs=[pl.BlockSpec((N_EXPERTS, tn), lambda i: (0, i)), _full_spec((N_EXPERTS, 1))],
        out_specs=pl.BlockSpec((tn, 2 * N_EXPERTS), lambda i: (i, 0)),
        out_shape=jax.ShapeDtypeStruct((n, 2 * N_EXPERTS), F32),
        compiler_params=_cparams(("arbitrary",)),
        name="router_topk",
    )(lgt, e_bias)


def _moe_kernel(h_ref, comb_ref, x1_ref, g2_ref, gfin_ref, wg_ref, wu_ref, wd_ref, o_ref, acc_ref,
                *, eb, final_norm):
    k = pl.program_id(1)

    @pl.when(k == 0)
    def _():
        acc_ref[...] = jnp.zeros_like(acc_ref)

    hb = h_ref[...]
    comb = comb_ref[...]
    lane = lax.broadcasted_iota(jnp.int32, comb.shape, 1)
    for j in range(eb):
        e = k * eb + j
        cw = jnp.sum(jnp.where(lane == e, comb, 0.0), axis=1, keepdims=True)
        hg = jnp.dot(hb, wg_ref[j], preferred_element_type=F32)
        hu = jnp.dot(hb, wu_ref[j], preferred_element_type=F32)
        act = (_silu(hg) * hu * cw).astype(BF16)
        acc_ref[...] += jnp.dot(act, wd_ref[j], preferred_element_type=F32)

    @pl.when(k == pl.num_programs(1) - 1)
    def _():
        y = x1_ref[...] + g2_ref[...] * acc_ref[...]
        if final_norm:
            y = _rms(y) * gfin_ref[...]
        o_ref[...] = y


def _moe_call(h2, comb, x1, g2, g_final, lw, tm, tiles_per_seq, final_norm, eb=5):
    n, d = x1.shape
    n_e = lw["w_gate"].shape[0]
    tps = tiles_per_seq
    if g2.ndim == 3:
        g2_spec = pl.BlockSpec((None, 1, d), lambda i, k: (i // tps, 0, 0))
    else:
        g2_spec = pl.BlockSpec((tm, d), lambda i, k: (i, 0))
    f = lw["w_gate"].shape[2]
    return pl.pallas_call(
        functools.partial(_moe_kernel, eb=eb, final_norm=final_norm),
        grid=(n // tm, n_e // eb),
        in_specs=[pl.BlockSpec((tm, d), lambda i, k: (i, 0)),
                  pl.BlockSpec((tm, comb.shape[1]), lambda i, k: (i, 0)),
                  pl.BlockSpec((tm, d), lambda i, k: (i, 0)),
                  g2_spec,
                  pl.BlockSpec((1, d), lambda i, k: (0, 0)),
                  pl.BlockSpec((eb, d, f), lambda i, k: (k, 0, 0)),
                  pl.BlockSpec((eb, d, f), lambda i, k: (k, 0, 0)),
                  pl.BlockSpec((eb, f, d), lambda i, k: (k, 0, 0))],
        out_specs=pl.BlockSpec((tm, d), lambda i, k: (i, 0)),
        out_shape=jax.ShapeDtypeStruct((n, d), F32),
        scratch_shapes=[pltpu.VMEM((tm, d), F32)],
        compiler_params=_cparams(("arbitrary", "arbitrary")),
        name="moe_experts",
    )(h2, comb, x1, g2, g_final, lw["w_gate"], lw["w_up"], lw["w_down"])


def _rope_table(positions, head_dim, period_lanes):
    half = head_dim // 2
    inv = np.float32(ROPE_THETA) ** (-np.arange(half, dtype=np.float32) / np.float32(half))
    lane = np.arange(LANES)
    active = lane < period_lanes
    within = lane % head_dim
    ang = (positions.astype(np.float32)[:, None] * inv[within % half][None, :]).astype(np.float64)
    cos = np.where(active[None, :], np.cos(ang), 1.0)
    sin = np.where(active[None, :], np.sin(ang), 0.0)
    s_lo = np.where((within < half)[None, :], -sin, 0.0)
    s_hi = np.where((within >= half)[None, :], sin, 0.0)
    return jnp.asarray(np.concatenate([cos, s_lo, s_hi], axis=1), F32)


def _layer_weights(l, w_in, g_mix, g_bq, w_uq, w_uk, w_uv, g_bkv, b_f, w_oa, w_ob, w_oc, w_out, g_ffn,
                   w_router, e_bias, w_gate, w_up, w_down, ws_gate, ws_up, ws_down):
    d = w_in.shape[1]
    wi = w_in[l]
    cols = []
    for h in range(A_HEADS):
        cols += [wi[:, h * 64:(h + 1) * 64], wi[:, 256 + h * 64:256 + (h + 1) * 64]]
    cols += [wi[:, 512:640], wi[:, 640:768], wi[:, 768:1024], wi[:, 1024:1152]]
    for j in range(4):
        cols += [wi[:, 1184 + j * 64:1184 + (j + 1) * 64], wi[:, 1184 + (4 + j) * 64:1184 + (5 + j) * 64]]
    cols += [wi[:, 1696:1824], wi[:, 1824:1952]]
    cols += [wi[:, 1152:1184], wi[:, 1952:1960], jnp.zeros((d, LANES - 40), wi.dtype)]
    cols += [wi[:, 1960:5032]]
    w_in_p = jnp.concatenate(cols, axis=1).astype(BF16)

    uq = w_uq[l]
    zpad = jnp.zeros((B_Q_LORA, B_HEADS, LANES - B_NOPE_DIM), uq.dtype)
    w_uqn = jnp.concatenate([uq[:, :, :B_NOPE_DIM], zpad], axis=2).reshape(B_Q_LORA, B_HEADS * LANES)
    zpad = jnp.zeros((B_Q_LORA, B_HEADS, LANES - B_ROPE_DIM), uq.dtype)
    w_uqr = jnp.concatenate([uq[:, :, B_NOPE_DIM:], zpad], axis=2).reshape(B_Q_LORA, B_HEADS * LANES)
    uk = jnp.transpose(w_uk[l], (1, 2, 0))
    uk = jnp.concatenate([uk, jnp.zeros((B_HEADS, LANES - B_NOPE_DIM, B_KV_LORA), uk.dtype)], axis=1)
    uv = w_uv[l]
    eye = jnp.eye(B_HEADS, dtype=uv.dtype)
    w_uv_bd = jnp.einsum("chv,hg->hcgv", uv, eye).reshape(B_HEADS * B_KV_LORA, B_HEADS * B_V_DIM)
    oc = w_oc[l].reshape(C_HEADS, C_HEAD_DIM, d)
    w_oc_p = jnp.concatenate([oc[h] for j in range(4) for h in (j, 4 + j)], axis=0)

    return {
        "w_in": w_in_p,
        "w_cft": jnp.transpose(wi[:, 1952:1960]).astype(BF16),
        "g_mix": g_mix[l][None, :], "b_f": b_f[l][None, :], "b_ft": b_f[l][:, None],
        "g_bq": g_bq[l][None, :], "w_uqn": w_uqn.astype(BF16), "w_uqr": w_uqr.astype(BF16),
        "w_uk": uk.astype(BF16), "g_bkv": g_bkv[l][None, :],
        "w_oa": w_oa[l].astype(BF16), "w_uv": w_uv_bd.astype(BF16), "w_ob": w_ob[l].astype(BF16),
        "w_oc": w_oc_p.astype(BF16), "w_oc_hm": w_oc[l].astype(BF16), "w_out": w_out[l].astype(BF16),
        "g_ffn": g_ffn[l][None, :],
        "w_rt": jnp.transpose(w_router[l]).astype(BF16), "e_bias": e_bias[l][:, None],
        "w_gate": jnp.concatenate([w_gate[l], ws_gate[l][None]], axis=0).astype(BF16),
        "w_up": jnp.concatenate([w_up[l], ws_up[l][None]], axis=0).astype(BF16),
        "w_down": jnp.concatenate([w_down[l], ws_down[l][None]], axis=0).astype(BF16),
    }


def _pad_rows(a, rows):
    return jnp.concatenate([a, jnp.zeros((a.shape[0], rows - a.shape[1]) + a.shape[2:], a.dtype)], axis=1)


def kernel(x_prompt, x_sample, cache_a_k, cache_a_v, cache_b_ckv, cache_b_krope, cache_c_k, cache_c_v,
           cache_c_logf, page_table, c_prompt, c_sample, w_ada, b_ada, g_mix, w_in, a_lambda, a_subln, g_bq,
           w_uq, w_uk, w_uv, g_bkv, b_f, w_oa, w_ob, w_oc, w_out, g_ffn, w_router, e_bias, w_gate, w_up,
           w_down, ws_gate, ws_up, ws_down, g_final):
    batch, seq, d = x_prompt.shape
    n_seq, n_new, _ = x_sample.shape
    depth, n_pool, page = cache_a_k.shape[:3]
    past_len = page_table.shape[1] * page
    n_p, n_s = batch * seq, n_seq * n_new

    tm_p, tq = 512, 256
    tiles_p = seq // tm_p

    akt = jnp.transpose(cache_a_k, (0, 1, 3, 4, 5, 2)).reshape(depth, n_pool, 128, page)
    av = cache_a_v.reshape(depth, n_pool, page, 128)
    krt = jnp.transpose(cache_b_krope, (0, 1, 3, 2))
    ckt = jnp.transpose(cache_c_k, (0, 1, 3, 4, 2)).reshape(depth, n_pool, 128, page)
    cvt = jnp.transpose(cache_c_v, (0, 1, 3, 4, 2)).reshape(depth, n_pool, 128, page)
    lft = jnp.transpose(cache_c_logf, (0, 1, 3, 2))
    caches = (akt, av, cache_b_ckv, krt, ckt, cvt, lft)

    pos_p = np.arange(seq)
    pos_s = np.tile(past_len + np.arange(n_new), n_seq)
    t64_p, t32_p = _rope_table(pos_p, A_HEAD_DIM, LANES), _rope_table(pos_p, B_ROPE_DIM, B_ROPE_DIM)
    t64_s, t32_s = _rope_table(pos_s, A_HEAD_DIM, LANES), _rope_table(pos_s, B_ROPE_DIM, B_ROPE_DIM)

    c_all = jnp.concatenate([c_prompt, c_sample], axis=0)
    mod = _mod_call(c_all, w_ada, b_ada)

    xp = x_prompt.reshape(n_p, d)
    xs = x_sample.reshape(n_s, d)
    g_fin = g_final[None, :]
    rows_p, rows_s = [], []
    for l in range(depth):
        lam_init = 0.8 - 0.6 * math.exp(-0.3 * l)
        lw = _layer_weights(l, w_in, g_mix, g_bq, w_uq, w_uk, w_uv, g_bkv, b_f, w_oa, w_ob, w_oc, w_out,
                            g_ffn, w_router, e_bias, w_gate, w_up, w_down, ws_gate, ws_up, ws_down)
        alam, subln = a_lambda[l], a_subln[l][None, :]
        mods = jnp.split(mod[l], 6, axis=-1)
        mp = [m[:batch][:, None, :] for m in mods]
        ms = [jnp.repeat(m[batch:], n_new, axis=0) for m in mods]
        final = l == depth - 1

        pr = _inproj_call(xp, mp[0], mp[1], lw, t64_p, t32_p, tm_p, tiles_p)
        oa = _flash_call("A", pr["aq"], pr["akb"], pr["avb"], (alam, subln), batch, seq, tq, lam_init)
        ol = _flash_call("B", pr["qb"], pr["kbb"], None, (), batch, seq, tq)
        oc = _flash_call("C", pr["cq"], pr["ckb"], pr["cvb"], (pr["lft"],), batch, seq, tq)
        x1, h2, lgt = _outproj_call(xp, oa, ol, oc, lw["w_oc"], pr["gl"], mp[2], mp[3], mp[4], lw, tm_p, tiles_p)
        comb = _router_call(lgt, lw["e_bias"], 512)
        xp = _moe_call(h2, comb, x1, mp[5], g_fin, lw, tm_p, tiles_p, final)
        rows_p.append((pr["ak"].reshape(batch, seq, 2, 1, 64), pr["av"].reshape(batch, seq, 1, 128),
                       pr["ckv"].reshape(batch, seq, 128), pr["kr"].reshape(batch, seq, 32),
                       pr["ck"].reshape(batch, seq, 2, 64), pr["cv"].reshape(batch, seq, 2, 64),
                       pr["lf"].reshape(batch, seq, 8)))

        sr = _inproj_call(xs, ms[0], ms[1], lw, t64_s, t32_s, n_s, None)
        aq = sr["aq"].reshape(n_seq, n_new, A_HEADS, 2, 64)
        z = jnp.zeros_like(aq[..., 0, :])
        qa = jnp.stack([jnp.concatenate([aq[..., 0, :], z], axis=-1),
                        jnp.concatenate([z, aq[..., 1, :]], axis=-1)], axis=1)
        qa = qa.reshape(n_seq, 2 * n_new * A_HEADS, 128)
        qb = sr["qb"].reshape(n_seq, n_new * B_HEADS, 256)
        qbl, qbr = qb[..., :128], qb[..., 128:128 + B_ROPE_DIM]
        cq = sr["cq"].reshape(n_seq, n_new, 4, 2, 64)
        z = jnp.zeros_like(cq[..., 0, :])
        qc = jnp.concatenate([jnp.concatenate([cq[..., 0, :], z], axis=-1),
                              jnp.concatenate([z, cq[..., 1, :]], axis=-1)], axis=2)
        qc = qc.reshape(n_seq, n_new * C_HEADS, 128)
        news = [_pad_rows(sr[k].reshape(n_seq, n_new, -1), NEW_PAD) for k in ("ak", "av", "ckv", "kr", "ck", "cv")]
        lfnt = jnp.transpose(_pad_rows(sr["lf"].reshape(n_seq, n_new, C_HEADS), NEW_PAD), (0, 2, 1))
        oa_s, ob_s, oc_s = _decode_call(l, page_table, qa, qbl, qbr, qc, news + [lfnt], alam, subln, caches,
                                        lam_init)
        oa_s = oa_s.reshape(n_s, A_HEADS * A_V_DIM).astype(BF16)
        ol_s = ob_s.reshape(n_s, B_HEADS * B_KV_LORA).astype(BF16)
        oc4 = oc_s.reshape(n_seq, n_new, C_HEADS, 2, 64)
        oc_s = jnp.concatenate([oc4[:, :, :4, 0, :], oc4[:, :, 4:, 1, :]], axis=2)
        oc_s = oc_s.reshape(n_s, C_HEADS * C_HEAD_DIM).astype(BF16)
        x1, h2, lgt = _outproj_call(xs, oa_s, ol_s, oc_s, lw["w_oc_hm"], sr["gl"], ms[2], ms[3], ms[4], lw, n_s,
                                    None)
        comb = _router_call(lgt, lw["e_bias"], n_s)
        xs = _moe_call(h2, comb, x1, ms[5], g_fin, lw, n_s, None, final)
        rows_s.append((sr["ak"].reshape(n_seq, n_new, 2, 1, 64), sr["av"].reshape(n_seq, n_new, 1, 128),
                       sr["ckv"].reshape(n_seq, n_new, 128), sr["kr"].reshape(n_seq, n_new, 32),
                       sr["ck"].reshape(n_seq, n_new, 2, 64), sr["cv"].reshape(n_seq, n_new, 2, 64),
                       sr["lf"].reshape(n_seq, n_new, 8)))

    outs_p = [jnp.stack(r) for r in zip(*rows_p)]
    outs_s = [jnp.stack(r) for r in zip(*rows_s)]
    return (xp.reshape(batch, seq, d), xs.reshape(n_seq, n_new, d), *outs_p, *outs_s)
```
